```python
import functools
import jax, jax.numpy as jnp
from jax import lax
import numpy as np

D_MODEL = 1024
BATCH = 16
SEQ = 2048
DEPTH = 1
DEC_BATCH = 128
DEC_SEQ = 1
PAST_LEN = 8192
PAGE_SIZE = 128

MIX_WIDTH = D_MODEL
SB_HEADS = 8
SB_HEAD_DIM = 64
SB_WIDTH = SB_HEADS * SB_HEAD_DIM
SB_BIAS_INIT = -6.0
LRU_WIDTH = MIX_WIDTH - SB_WIDTH
LRU_BLOCKS = 8
LRU_BLOCK_DIM = LRU_WIDTH // LRU_BLOCKS
LRU_C = 8.0
CONV_WIDTH = 4
Q_BLOCK = 128
IN_WIDTH = 3 * SB_WIDTH + 2 * LRU_WIDTH
PEER_HEADS = 8
PEER_N_KEYS = 128
PEER_N_EXPERTS = PEER_N_KEYS * PEER_N_KEYS
PEER_TOPK = 16
PEER_QUERY_DIM = 256
PEER_HALF = PEER_QUERY_DIM // 2
PEER_BLOCK = 128
PLE_DIM = 256
RMS_EPS = 1e-6

kernel_name = 'hymba_stickbreak_rglru_peer_step'


def _rmsnorm(x, g):
    xf = x.astype(jnp.float32)
    y = xf * lax.rsqrt(jnp.mean(xf * xf, axis=-1, keepdims=True) + RMS_EPS) * g.astype(jnp.float32)
    return y.astype(x.dtype)


def _sb_weights(z, valid, carry):
    log_keep = jnp.where(valid, jax.nn.log_sigmoid(-z), 0.0)
    newer = lax.cumsum(log_keep, axis=z.ndim - 1, reverse=True) - log_keep + carry[..., None]
    w = jnp.where(valid, jnp.exp(jax.nn.log_sigmoid(z) + newer), 0.0)
    return w, carry + jnp.sum(log_keep, axis=-1)


def _sb_prompt(q, k, v, bias):
    b, s = q.shape[0], q.shape[1]
    n_blk = s // Q_BLOCK
    scale = SB_HEAD_DIM ** -0.5
    q_blk = q.reshape(b, n_blk, Q_BLOCK, SB_HEADS, SB_HEAD_DIM).transpose(1, 0, 2, 3, 4)
    vf = v.astype(jnp.float32)
    bf = bias.astype(jnp.float32)[:, None, None]
    k_pos = jnp.arange(s)

    def one_block(args):
        qb, start = args
        z = jnp.einsum('bthd,bshd->bhts', qb, k, preferred_element_type=jnp.float32) * scale + bf
        t_pos = start + jnp.arange(Q_BLOCK)
        valid = k_pos[None, :] < t_pos[:, None]
        w, _ = _sb_weights(z, valid, jnp.zeros(z.shape[:-1], jnp.float32))
        return jnp.einsum('bhts,bshd->bthd', w, vf)

    o = lax.map(one_block, (q_blk, jnp.arange(n_blk) * Q_BLOCK))
    return o.transpose(1, 0, 2, 3, 4).reshape(b, s, SB_WIDTH).astype(q.dtype)


def _sb_sample(q, k, v, bias, cache_k, cache_v, page_table, layer):
    b, t = q.shape[0], q.shape[1]
    scale = SB_HEAD_DIM ** -0.5
    bf = bias.astype(jnp.float32)[:, None, None]
    z = jnp.einsum('bthd,bshd->bhts', q, k, preferred_element_type=jnp.float32) * scale + bf
    idx = jnp.arange(t)
    w, carry = _sb_weights(z, idx[None, :] < idx[:, None], jnp.zeros(z.shape[:-1], jnp.float32))
    o = jnp.einsum('bhts,bshd->bthd', w, v.astype(jnp.float32))

    def page_step(c, pages):
        o_acc, log_keep_acc = c
        kp = cache_k[layer, pages]
        vp = cache_v[layer, pages]
        zp = jnp.einsum('bthd,bshd->bhts', q, kp, preferred_element_type=jnp.float32) * scale + bf
        wp, log_keep_acc = _sb_weights(zp, True, log_keep_acc)
        o_acc = o_acc + jnp.einsum('bhts,bshd->bthd', wp, vp.astype(jnp.float32))
        return (o_acc, log_keep_acc), None

    (o, _), _ = lax.scan(page_step, (o, carry), page_table.T[::-1])
    return o.reshape(b, t, SB_WIDTH).astype(q.dtype)


def _causal_conv(x, buf, w, b):
    t = x.shape[1]
    xp = jnp.concatenate([buf.astype(x.dtype), x], axis=1)
    out = b
    for j in range(CONV_WIDTH):
        out = out + xp[:, j:j + t] * w[j]
    return out, xp[:, t:]


def _rglru(x, h0, gate_a_w, gate_a_b, gate_x_w, gate_x_b, lru_lambda):
    b, t, _ = x.shape
    xf = x.astype(jnp.float32)
    xb = xf.reshape(b, t, LRU_BLOCKS, LRU_BLOCK_DIM)
    r = jax.nn.sigmoid(jnp.einsum('btnd,nde->btne', xb, gate_a_w.astype(jnp.float32))
                       + gate_a_b.astype(jnp.float32)).reshape(b, t, LRU_WIDTH)
    i = jax.nn.sigmoid(jnp.einsum('btnd,nde->btne', xb, gate_x_w.astype(jnp.float32))
                       + gate_x_b.astype(jnp.float32)).reshape(b, t, LRU_WIDTH)
    log_a = -LRU_C * r * jax.nn.softplus(-lru_lambda.astype(jnp.float32))
    a = jnp.exp(log_a)
    u = jnp.sqrt(-jnp.expm1(2.0 * log_a)) * (i * xf)

    def step(h, inp):
        a_t, u_t = inp
        h = a_t * h + u_t
        return h, h

    h_last, hs = lax.scan(step, h0.astype(jnp.float32), (a.transpose(1, 0, 2), u.transpose(1, 0, 2)))
    return hs.transpose(1, 0, 2), h_last


def _peer(xn, w_q, keys1, keys2, u, v):
    shape = xn.shape
    xf = xn.reshape(-1, D_MODEL)
    n = xf.shape[0]
    n_blk = -(-n // PEER_BLOCK)
    xf = jnp.pad(xf, ((0, n_blk * PEER_BLOCK - n), (0, 0))).reshape(n_blk, PEER_BLOCK, D_MODEL)
    k1 = keys1.astype(jnp.float32)
    k2 = keys2.astype(jnp.float32)

    def one_block(xb):
        q = (xb @ w_q).astype(jnp.float32).reshape(PEER_BLOCK, PEER_HEADS, 2, PEER_HALF)
        s1 = jnp.einsum('thd,kd->thk', q[:, :, 0], k1)
        s2 = jnp.einsum('thd,kd->thk', q[:, :, 1], k2)
        v1, i1 = lax.top_k(s1, PEER_TOPK)
        v2, i2 = lax.top_k(s2, PEER_TOPK)
        cand = (v1[..., :, None] + v2[..., None, :]).reshape(PEER_BLOCK, PEER_HEADS, PEER_TOPK * PEER_TOPK)
        sc, ci = lax.top_k(cand, PEER_TOPK)
        e1 = jnp.take_along_axis(i1, ci // PEER_TOPK, axis=-1)
        e2 = jnp.take_along_axis(i2, ci % PEER_TOPK, axis=-1)
        idx = e1 * PEER_N_KEYS + e2
        g = jax.nn.softmax(sc, axis=-1)
        act = jax.nn.gelu(jnp.einsum('thkd,td->thk', u[idx], xb, preferred_element_type=jnp.float32))
        out = jnp.einsum('thk,thkd->td', g * act, v[idx], preferred_element_type=jnp.float32)
        return out.astype(xn.dtype)

    out = lax.map(one_block, xf).reshape(-1, D_MODEL)[:n]
    return out.reshape(shape)


def _layer(x, p, sb_fn, conv_buf, h0, norm_mix, w_in, sb_bias, conv_w, conv_b, gate_a_w, gate_a_b,
           gate_x_w, gate_x_b, lru_lambda, norm_attn_out, norm_lru_out, w_out, norm_ffn,
           peer_w_q, peer_keys1, peer_keys2, peer_u, peer_v, norm_ple, ple_w_gate, ple_w_proj):
    b, t, _ = x.shape
    xn = _rmsnorm(x, norm_mix)
    proj = xn @ w_in
    q, k, v, xl, gl = jnp.split(proj, [SB_WIDTH, 2 * SB_WIDTH, 3 * SB_WIDTH, 3 * SB_WIDTH + LRU_WIDTH], axis=-1)
    q = q.reshape(b, t, SB_HEADS, SB_HEAD_DIM)
    k = k.reshape(b, t, SB_HEADS, SB_HEAD_DIM)
    v = v.reshape(b, t, SB_HEADS, SB_HEAD_DIM)
    o_att = sb_fn(q, k, v, sb_bias)
    xc, conv_new = _causal_conv(xl, conv_buf, conv_w, conv_b)
    hs, h_last = _rglru(xc, h0, gate_a_w, gate_a_b, gate_x_w, gate_x_b, lru_lambda)
    o_lru = (hs * jax.nn.gelu(gl.astype(jnp.float32))).astype(x.dtype)
    mix = jnp.concatenate([_rmsnorm(o_att, norm_attn_out), _rmsnorm(o_lru, norm_lru_out)], axis=-1) @ w_out
    x = x + mix
    x = x + _peer(_rmsnorm(x, norm_ffn), peer_w_q, peer_keys1, peer_keys2, peer_u, peer_v)
    gate = jax.nn.sigmoid(_rmsnorm(x, norm_ple) @ ple_w_gate)
    x = x + gate * (p @ ple_w_proj)
    return x, k, v, conv_new, h_last


def setup_inputs(seed: int = 0) -> dict:
    key = jax.random.key(seed)
    k = jax.random.split(key, 32)
    f32 = jnp.float32

    def nrm(kk, shape, scale):
        return jax.random.normal(kk, shape, f32) * scale

    def gain(kk, shape):
        return 1.0 + 0.02 * jax.random.normal(kk, shape, f32)

    n_pages = PAST_LEN // PAGE_SIZE
    n_used = DEC_BATCH * n_pages
    n_phys = n_used + max(1, n_used // 4)
    page_table = jax.random.permutation(k[6], n_phys)[:n_used].reshape(DEC_BATCH, n_pages).astype(jnp.int32)
    a_c = jax.random.uniform(k[17], (DEPTH, LRU_WIDTH), f32, 0.9, 0.999)
    s_a = a_c ** (1.0 / LRU_C)
    lru_lambda = jnp.log(s_a) - jnp.log1p(-s_a)
    return {
        'x_prompt': nrm(k[0], (BATCH, SEQ, D_MODEL), 1.0),
        'x_sample': nrm(k[1], (DEC_BATCH, DEC_SEQ, D_MODEL), 1.0),
        'cache_k': nrm(k[2], (DEPTH, n_phys, PAGE_SIZE, SB_HEADS, SB_HEAD_DIM), 1.0),
        'cache_v': nrm(k[3], (DEPTH, n_phys, PAGE_SIZE, SB_HEADS, SB_HEAD_DIM), 1.0),
        'state_conv': nrm(k[4], (DEPTH, DEC_BATCH, CONV_WIDTH - 1, LRU_WIDTH), 1.0),
        'state_h': nrm(k[5], (DEPTH, DEC_BATCH, LRU_WIDTH), 0.5),
        'page_table': page_table,
        'p_prompt': nrm(k[7], (DEPTH, BATCH, SEQ, PLE_DIM), 1.0),
        'p_sample': nrm(k[8], (DEPTH, DEC_BATCH, DEC_SEQ, PLE_DIM), 1.0),
        'norm_mix': gain(k[9], (DEPTH, D_MODEL)),
        'w_in': nrm(k[10], (DEPTH, D_MODEL, IN_WIDTH), D_MODEL ** -0.5),
        'sb_bias': SB_BIAS_INIT + 0.1 * jax.random.normal(k[31], (DEPTH, SB_HEADS), f32),
        'conv_w': nrm(k[11], (DEPTH, CONV_WIDTH, LRU_WIDTH), CONV_WIDTH ** -0.5),
        'conv_b': nrm(k[12], (DEPTH, LRU_WIDTH), 0.01),
        'gate_a_w': nrm(k[13], (DEPTH, LRU_BLOCKS, LRU_BLOCK_DIM, LRU_BLOCK_DIM), LRU_BLOCK_DIM ** -0.5),
        'gate_a_b': nrm(k[14], (DEPTH, LRU_BLOCKS, LRU_BLOCK_DIM), 0.01),
        'gate_x_w': nrm(k[15], (DEPTH, LRU_BLOCKS, LRU_BLOCK_DIM, LRU_BLOCK_DIM), LRU_BLOCK_DIM ** -0.5),
        'gate_x_b': nrm(k[16], (DEPTH, LRU_BLOCKS, LRU_BLOCK_DIM), 0.01),
        'lru_lambda': lru_lambda,
        'norm_attn_out': gain(k[18], (DEPTH, SB_WIDTH)),
        'norm_lru_out': gain(k[19], (DEPTH, LRU_WIDTH)),
        'w_out': nrm(k[20], (DEPTH, MIX_WIDTH, D_MODEL), MIX_WIDTH ** -0.5),
        'norm_ffn': gain(k[21], (DEPTH, D_MODEL)),
        'peer_w_q': nrm(k[22], (DEPTH, D_MODEL, PEER_HEADS * PEER_QUERY_DIM), D_MODEL ** -0.5),
        'peer_keys1': nrm(k[23], (DEPTH, PEER_N_KEYS, PEER_HALF), PEER_HALF ** -0.5),
        'peer_keys2': nrm(k[24], (DEPTH, PEER_N_KEYS, PEER_HALF), PEER_HALF ** -0.5),
        'peer_u': nrm(k[25], (DEPTH, PEER_N_EXPERTS, D_MODEL), D_MODEL ** -0.5),
        'peer_v': nrm(k[26], (DEPTH, PEER_N_EXPERTS, D_MODEL), D_MODEL ** -0.5),
        'norm_ple': gain(k[27], (DEPTH, D_MODEL)),
        'ple_w_gate': nrm(k[28], (DEPTH, D_MODEL, D_MODEL), D_MODEL ** -0.5),
        'ple_w_proj': nrm(k[29], (DEPTH, PLE_DIM, D_MODEL), PLE_DIM ** -0.5),
        'norm_final': gain(k[30], (D_MODEL,)),
    }


def reference(x_prompt, x_sample, cache_k, cache_v, state_conv, state_h, page_table, p_prompt, p_sample,
              norm_mix, w_in, sb_bias, conv_w, conv_b, gate_a_w, gate_a_b, gate_x_w, gate_x_b, lru_lambda,
              norm_attn_out, norm_lru_out, w_out, norm_ffn, peer_w_q, peer_keys1, peer_keys2,
              peer_u, peer_v, norm_ple, ple_w_gate, ple_w_proj, norm_final):
    hp, hs = x_prompt, x_sample
    bp = x_prompt.shape[0]
    kp_l, vp_l, cp_l, hp_l = [], [], [], []
    ks_l, vs_l, cs_l, hs_l = [], [], [], []
    for i in range(DEPTH):
        lw = (norm_mix[i], w_in[i], sb_bias[i], conv_w[i], conv_b[i], gate_a_w[i], gate_a_b[i], gate_x_w[i],
              gate_x_b[i], lru_lambda[i], norm_attn_out[i], norm_lru_out[i], w_out[i], norm_ffn[i],
              peer_w_q[i], peer_keys1[i], peer_keys2[i], peer_u[i], peer_v[i], norm_ple[i],
              ple_w_gate[i], ple_w_proj[i])
        conv0 = jnp.zeros((bp, CONV_WIDTH - 1, LRU_WIDTH), x_prompt.dtype)
        h0 = jnp.zeros((bp, LRU_WIDTH), jnp.float32)
        hp, k_new, v_new, c_new, h_new = _layer(hp, p_prompt[i], _sb_prompt, conv0, h0, *lw)
        kp_l.append(k_new); vp_l.append(v_new); cp_l.append(c_new); hp_l.append(h_new)
        sb_fn = functools.partial(_sb_sample, cache_k=cache_k, cache_v=cache_v,
                                  page_table=page_table, layer=i)
        hs, k_new, v_new, c_new, h_new = _layer(hs, p_sample[i], sb_fn, state_conv[i], state_h[i], *lw)
        ks_l.append(k_new); vs_l.append(v_new); cs_l.append(c_new); hs_l.append(h_new)
    y_prompt = _rmsnorm(hp, norm_final)
    y_sample = _rmsnorm(hs, norm_final)
    return (y_prompt, y_sample,
            jnp.stack(kp_l), jnp.stack(vp_l), jnp.stack(cp_l), jnp.stack(hp_l),
            jnp.stack(ks_l), jnp.stack(vs_l), jnp.stack(cs_l), jnp.stack(hs_l))
```

```python
import functools
import math

import numpy as np
import jax
import jax.numpy as jnp
from jax import lax
from jax.experimental import pallas as pl
from jax.experimental.pallas import tpu as pltpu

F32 = jnp.float32
BF16 = jnp.bfloat16

RMS_EPS = 1e-6
LRU_C = 8.0
CONV_WIDTH = 4
PEER_TOPK = 16
LOG2E = 1.4426950408889634
NEG_INF = float("-inf")

LANES = 128
SUBLANES = 8
VMEM_LIMIT = 56 * 1024 * 1024


def _cparams(sem):
    return pltpu.CompilerParams(dimension_semantics=sem, vmem_limit_bytes=VMEM_LIMIT)


def _rms(x, g):
    ms = jnp.mean(x * x, axis=-1, keepdims=True)
    return x * lax.rsqrt(ms + RMS_EPS) * g


def _softplus(z):
    return jnp.maximum(z, 0.0) + jnp.log1p(jnp.exp(-jnp.abs(z)))


def _gelu_tanh(x):
    c = math.sqrt(2.0 / math.pi)
    return 0.5 * x * (1.0 + jnp.tanh(c * (x + 0.044715 * (x * x * x))))


def _dot(a, b):
    return jnp.dot(a, b, preferred_element_type=F32)


def _dot_nt(a, b):
    return lax.dot_general(a, b, (((1,), (1,)), ((), ())), preferred_element_type=F32)


def _split_bf16(x):
    hi = x.astype(BF16)
    lo = (x - hi.astype(F32)).astype(BF16)
    return hi, lo


def _full_spec(shape):
    nd = len(shape)
    return pl.BlockSpec(shape, lambda *_: (0,) * nd)


def _in_proj_kernel(x_ref, g_ref, w_ref, q_ref, k_ref, v_ref, xl_ref, gl_ref):
    xn = _rms(x_ref[...], g_ref[...]).astype(BF16)
    proj = _dot(xn, w_ref[...])
    width = q_ref.shape[-1]
    for i, ref in enumerate((q_ref, k_ref, v_ref, xl_ref, gl_ref)):
        ref[...] = proj[:, i * width:(i + 1) * width]


def _in_proj(x2d, norm_g, w_in_bf, tm):
    n, d = x2d.shape
    width = w_in_bf.shape[1] // 5
    out = jax.ShapeDtypeStruct((n, width), F32)
    return pl.pallas_call(
        _in_proj_kernel,
        grid=(n // tm,),
        in_specs=[pl.BlockSpec((tm, d), lambda i: (i, 0)),
                  _full_spec((1, d)),
                  _full_spec(w_in_bf.shape)],
        out_specs=[pl.BlockSpec((tm, width), lambda i: (i, 0))] * 5,
        out_shape=[out] * 5,
        compiler_params=_cparams(("parallel",)),
        name="in_proj",
    )(x2d, norm_g.reshape(1, d), w_in_bf)


def _sb_prompt_kernel(bias_ref, q_ref, k_ref, v_ref, tri_ref, o_ref, *, tq, head_dim):
    hp = pl.program_id(1)
    i = pl.program_id(2)
    scale = head_dim ** -0.5
    lane = lax.broadcasted_iota(jnp.int32, (1, LANES), 1)
    q = q_ref[...] * scale
    tri = tri_ref[...]
    row = lax.broadcasted_iota(jnp.int32, (tq, tq), 0)
    col = lax.broadcasted_iota(jnp.int32, (tq, tq), 1)
    diag_valid = col < row
    out = jnp.zeros((tq, LANES), F32)
    for hh in range(LANES // head_dim):
        in_head = (lane >= hh * head_dim) & (lane < (hh + 1) * head_dim)
        bias = bias_ref[hp * (LANES // head_dim) + hh]
        qh = jnp.where(in_head, q, 0.0).astype(BF16)

        def block(j, o_acc, carry, masked):
            start = pl.multiple_of(j * tq, tq)
            kblk = k_ref[pl.ds(start, tq), :].astype(BF16)
            vblk = v_ref[pl.ds(start, tq), :].astype(BF16)
            z = _dot_nt(qh, kblk) + bias
            lk = -_softplus(z)
            if masked:
                lk = jnp.where(diag_valid, lk, 0.0)
            hi, lo = _split_bf16(lk)
            r_incl = _dot(hi, tri) + _dot(lo, tri)
            w = jnp.exp(z + r_incl + carry)
            if masked:
                w = jnp.where(diag_valid, w, 0.0)
            o_acc = o_acc + _dot(w.astype(BF16), vblk)
            carry = carry + jnp.sum(lk, axis=1, keepdims=True)
            return o_acc, carry

        o_acc, carry = block(i, jnp.zeros((tq, LANES), F32), jnp.zeros((tq, 1), F32), True)

        def body(jj, c):
            return block(i - 1 - jj, c[0], c[1], False)

        o_acc, _ = lax.fori_loop(0, i, body, (o_acc, carry))
        out = jnp.where(in_head, o_acc, out)
    o_ref[...] = out


def _sb_prompt(q, k, v, sb_bias, batch, seq, head_dim, tq):
    n, width = q.shape
    nblk = seq // tq
    npair = width // LANES
    tri = jnp.asarray(np.tril(np.ones((tq, tq), np.float32)), BF16)
    kern = functools.partial(_sb_prompt_kernel, tq=tq, head_dim=head_dim)
    return pl.pallas_call(
        kern,
        grid=(batch, npair, nblk),
        in_specs=[pl.BlockSpec(memory_space=pltpu.SMEM),
                  pl.BlockSpec((tq, LANES), lambda b, h, i: (b * nblk + i, h)),
                  pl.BlockSpec((seq, LANES), lambda b, h, i: (b, h)),
                  pl.BlockSpec((seq, LANES), lambda b, h, i: (b, h)),
                  _full_spec((tq, tq))],
        out_specs=pl.BlockSpec((tq, LANES), lambda b, h, i: (b * nblk + i, h)),
        out_shape=jax.ShapeDtypeStruct((n, width), F32),
        compiler_params=_cparams(("parallel", "parallel", "arbitrary")),
        name="sb_prompt",
    )(sb_bias, q, k, v, tri)


def _sb_sample_kernel(pt_ref, bias_ref, q_ref, tri_ref, *refs, pp, heads, head_dim, page):
    k_refs = refs[:pp]
    v_refs = refs[pp:2 * pp]
    o_ref = refs[2 * pp]
    qblk_ref, acc_ref, carry_ref = refs[2 * pp + 1:]
    j = pl.program_id(1)
    width = heads * head_dim
    hrow = lax.broadcasted_iota(jnp.int32, (heads, width), 0)
    lcol = lax.broadcasted_iota(jnp.int32, (heads, width), 1)
    blockdiag = (lcol >= hrow * head_dim) & (lcol < (hrow + 1) * head_dim)

    @pl.when(j == 0)
    def _():
        qrow = q_ref[...] * (head_dim ** -0.5)
        qb = jnp.where(blockdiag, jnp.broadcast_to(qrow, (heads, width)), 0.0)
        qblk_ref[...] = jnp.concatenate([qb, jnp.zeros_like(qb)], axis=0).astype(BF16)
        acc_ref[...] = jnp.zeros_like(acc_ref)
        carry_ref[...] = jnp.zeros_like(carry_ref)

    qblk = qblk_ref[...]
    bias_col = bias_ref[...]
    zs = []
    for p in range(pp):
        kp = k_refs[p][...].astype(BF16)
        zs.append(_dot_nt(qblk, kp)[:heads] + bias_col)
    z = jnp.concatenate(zs, axis=0)
    lk = -_softplus(z)
    hi, lo = _split_bf16(lk)
    tri = tri_ref[...]
    r_incl = _dot(hi, tri) + _dot(lo, tri)
    tot = jnp.sum(lk, axis=1, keepdims=True)
    carry = carry_ref[...]
    acc = acc_ref[...]
    for p in range(pp):
        sl = slice(p * heads, (p + 1) * heads)
        w = jnp.exp(z[sl] + r_incl[sl] + carry)
        w2 = jnp.concatenate([w, jnp.zeros_like(w)], axis=0).astype(BF16)
        vp = v_refs[p][...].astype(BF16)
        acc = acc + _dot(w2, vp)[:heads]
        carry = carry + tot[sl]
    acc_ref[...] = acc
    carry_ref[...] = carry

    @pl.when(j == pl.num_programs(1) - 1)
    def _():
        o_ref[...] = jnp.sum(jnp.where(blockdiag, acc, 0.0), axis=0, keepdims=True)


def _sb_sample(q, sb_bias, cache_k, cache_v, page_table, heads, head_dim, pp):
    bd, width = q.shape
    n_phys, page = cache_k.shape[0], cache_k.shape[1]
    n_pages = page_table.shape[1]
    ck = cache_k.reshape(n_phys, page, width)
    cv = cache_v.reshape(n_phys, page, width)
    tri = jnp.asarray(np.tril(np.ones((page, page), np.float32)), BF16)
    kern = functools.partial(_sb_sample_kernel, pp=pp, heads=heads, head_dim=head_dim, page=page)

    def page_map(p):
        return lambda b, j, pt: (pt[b, n_pages - 1 - (j * pp + p)], 0, 0)

    kv_specs = [pl.BlockSpec((None, page, width), page_map(p)) for p in range(pp)]
    grid_spec = pltpu.PrefetchScalarGridSpec(
        num_scalar_prefetch=1,
        grid=(bd, n_pages // pp),
        in_specs=[pl.BlockSpec((heads, 1), lambda b, j, pt: (0, 0)),
                  pl.BlockSpec((None, 1, width), lambda b, j, pt: (b, 0, 0)),
                  pl.BlockSpec((page, page), lambda b, j, pt: (0, 0))] + kv_specs + kv_specs,
        out_specs=pl.BlockSpec((None, 1, width), lambda b, j, pt: (b, 0, 0)),
        scratch_shapes=[pltpu.VMEM((2 * heads, width), BF16),
                        pltpu.VMEM((heads, width), F32),
                        pltpu.VMEM((heads, 1), F32)],
    )
    out = pl.pallas_call(
        kern,
        grid_spec=grid_spec,
        out_shape=jax.ShapeDtypeStruct((bd, 1, width), F32),
        compiler_params=_cparams(("parallel", "arbitrary")),
        name="sb_sample",
    )(page_table, sb_bias.reshape(heads, 1), q.reshape(bd, 1, width), tri,
      *([ck] * pp), *([cv] * pp))
    return out.reshape(bd, width)


def _lru_gates(xc, wa_ref, ba_ref, wx_ref, bx_ref, lam_ref):
    xcb = xc.astype(BF16)
    r = jax.nn.sigmoid(_dot(xcb, wa_ref[...]) + ba_ref[...])
    ig = jax.nn.sigmoid(_dot(xcb, wx_ref[...]) + bx_ref[...])
    log_a = (-LRU_C) * r * _softplus(-lam_ref[...])
    a = jnp.exp(log_a)
    one_minus_a2 = -jnp.tanh(log_a) * (a * a + 1.0)
    u = jnp.sqrt(one_minus_a2) * (ig * xc)
    return a, u


def _lru_prompt_kernel(xl_ref, gl_ref, conv0_ref, h0_ref, cw_ref, cb_ref, wa_ref, ba_ref, wx_ref, bx_ref,
                       lam_ref, o_ref, convn_ref, hn_ref, xpad_ref, h_ref, *, ts):
    tb = pl.program_id(1)
    tail = CONV_WIDTH - 1

    @pl.when(tb == 0)
    def _():
        xpad_ref[SUBLANES - tail:SUBLANES, :] = conv0_ref[...]
        h_ref[...] = h0_ref[...]

    xl = xl_ref[...]
    xpad_ref[SUBLANES:SUBLANES + ts, :] = xl
    cw = cw_ref[...]
    xc = cb_ref[...]
    for jw in range(CONV_WIDTH):
        off = SUBLANES - tail + jw
        xc = xc + xpad_ref[off:off + ts, :] * cw[jw:jw + 1, :]
    a, u = _lru_gates(xc, wa_ref, ba_ref, wx_ref, bx_ref, lam_ref)
    rows = lax.broadcasted_iota(jnp.int32, a.shape, 0)
    d = 1
    while d < ts:
        keep = rows >= d
        a_sh = jnp.where(keep, pltpu.roll(a, d, 0), 1.0)
        u_sh = jnp.where(keep, pltpu.roll(u, d, 0), 0.0)
        u = u + a * u_sh
        a = a * a_sh
        d *= 2
    hs = a * h_ref[...] + u
    o_ref[...] = hs * _gelu_tanh(gl_ref[...])
    h_ref[...] = hs[ts - 1:ts, :]
    xpad_ref[SUBLANES - tail:SUBLANES, :] = xl[ts - tail:ts, :]

    @pl.when(tb == pl.num_programs(1) - 1)
    def _():
        convn_ref[...] = xl[ts - tail:ts, :]
        hn_ref[...] = hs[ts - 1:ts, :]


def _lru_weights(conv_w, conv_b, gate_a_w, gate_a_b, gate_x_w, gate_x_b, lru_lambda):
    nb, bd, _ = gate_a_w.shape
    w = nb * bd
    eye = jnp.eye(nb, dtype=F32)
    wa = jnp.einsum('nde,nm->ndme', gate_a_w, eye).reshape(w, w).astype(BF16)
    wx = jnp.einsum('nde,nm->ndme', gate_x_w, eye).reshape(w, w).astype(BF16)
    return (conv_w, conv_b.reshape(1, w), wa, gate_a_b.reshape(1, w), wx, gate_x_b.reshape(1, w),
            lru_lambda.reshape(1, w))


def _lru_prompt(xl, gl, conv0, h0, lw, batch, seq, ts):
    n, w = xl.shape
    nblk = seq // ts
    tail = CONV_WIDTH - 1
    kern = functools.partial(_lru_prompt_kernel, ts=ts)
    tok = pl.BlockSpec((ts, w), lambda b, t: (b * nblk + t, 0))
    return pl.pallas_call(
        kern,
        grid=(batch, nblk),
        in_specs=[tok, tok,
                  pl.BlockSpec((None, tail, w), lambda b, t: (b, 0, 0)),
                  pl.BlockSpec((None, 1, w), lambda b, t: (b, 0, 0)),
                  _full_spec((CONV_WIDTH, w)), _full_spec((1, w)),
                  _full_spec((w, w)), _full_spec((1, w)),
                  _full_spec((w, w)), _full_spec((1, w)), _full_spec((1, w))],
        out_specs=[tok,
                   pl.BlockSpec((None, tail, w), lambda b, t: (b, 0, 0)),
                   pl.BlockSpec((None, 1, w), lambda b, t: (b, 0, 0))],
        out_shape=[jax.ShapeDtypeStruct((n, w), F32),
                   jax.ShapeDtypeStruct((batch, tail, w), F32),
                   jax.ShapeDtypeStruct((batch, 1, w), F32)],
        scratch_shapes=[pltpu.VMEM((SUBLANES + ts, w), F32), pltpu.VMEM((1, w), F32)],
        compiler_params=_cparams(("parallel", "arbitrary")),
        name="lru_prompt",
    )(xl, gl, conv0, h0.reshape(batch, 1, w), *lw)


def _lru_sample_kernel(xl_ref, gl_ref, c0_ref, c1_ref, c2_ref, h0_ref, cw_ref, cb_ref, wa_ref, ba_ref,
                       wx_ref, bx_ref, lam_ref, o_ref, hn_ref):
    cw = cw_ref[...]
    xl = xl_ref[...]
    xc = cb_ref[...] + c0_ref[...] * cw[0:1, :]
    xc = xc + c1_ref[...] * cw[1:2, :]
    xc = xc + c2_ref[...] * cw[2:3, :]
    xc = xc + xl * cw[3:4, :]
    a, u = _lru_gates(xc, wa_ref, ba_ref, wx_ref, bx_ref, lam_ref)
    h = a * h0_ref[...] + u
    hn_ref[...] = h
    o_ref[...] = h * _gelu_tanh(gl_ref[...])


def _lru_sample(xl, gl, state_conv, h0, lw):
    bd, w = xl.shape
    full = _full_spec((bd, w))
    return pl.pallas_call(
        _lru_sample_kernel,
        grid=(1,),
        in_specs=[full] * 6 + [_full_spec((CONV_WIDTH, w)), _full_spec((1, w)),
                               _full_spec((w, w)), _full_spec((1, w)),
                               _full_spec((w, w)), _full_spec((1, w)), _full_spec((1, w))],
        out_specs=[full, full],
        out_shape=[jax.ShapeDtypeStruct((bd, w), F32)] * 2,
        compiler_params=_cparams(("arbitrary",)),
        name="lru_sample",
    )(xl, gl, state_conv[:, 0], state_conv[:, 1], state_conv[:, 2], h0, *lw)


def _mix_peer_q_kernel(oa_ref, ol_ref, x_ref, ga_ref, gl_ref, wa_ref, wl_ref, gf_ref, wq_ref, k1_ref, k2_ref,
                       x1_ref, xn_ref, s1_ref, s2_ref, *, heads, half):
    na = _rms(oa_ref[...], ga_ref[...]).astype(BF16)
    nl = _rms(ol_ref[...], gl_ref[...]).astype(BF16)
    x1 = x_ref[...] + (_dot(na, wa_ref[...]) + _dot(nl, wl_ref[...]))
    x1_ref[...] = x1
    xn = _rms(x1, gf_ref[...]).astype(BF16)
    xn_ref[...] = xn
    qp = _dot(xn, wq_ref[...]).astype(BF16)
    k1 = k1_ref[...]
    k2 = k2_ref[...]
    for h in range(heads):
        base = h * 2 * half
        s1_ref[h] = _dot_nt(k1, qp[:, base:base + half])
        s2_ref[h] = _dot_nt(k2, qp[:, base + half:base + 2 * half])


def _mix_peer_q(o_att, o_lru, x2d, g_att, g_lru, w_out_bf, g_ffn, wq_bf, k1_bf, k2_bf, heads, tm):
    n, d = x2d.shape
    wa_w = o_att.shape[1]
    nk, half = k1_bf.shape
    kern = functools.partial(_mix_peer_q_kernel, heads=heads, half=half)
    tok = lambda w: pl.BlockSpec((tm, w), lambda i: (i, 0))
    sspec = pl.BlockSpec((heads, nk, tm), lambda i: (0, 0, i))
    return pl.pallas_call(
        kern,
        grid=(n // tm,),
        in_specs=[tok(wa_w), tok(o_lru.shape[1]), tok(d),
                  _full_spec((1, wa_w)), _full_spec((1, o_lru.shape[1])),
                  _full_spec((wa_w, d)), _full_spec((o_lru.shape[1], d)),
                  _full_spec((1, d)), _full_spec(wq_bf.shape),
                  _full_spec(k1_bf.shape), _full_spec(k2_bf.shape)],
        out_specs=[tok(d), tok(d), sspec, sspec],
        out_shape=[jax.ShapeDtypeStruct((n, d), F32), jax.ShapeDtypeStruct((n, d), BF16),
                   jax.ShapeDtypeStruct((heads, nk, n), F32), jax.ShapeDtypeStruct((heads, nk, n), F32)],
        compiler_params=_cparams(("parallel",)),
        name="mix_peer_q",
    )(o_att, o_lru, x2d, g_att.reshape(1, -1), g_lru.reshape(1, -1),
      w_out_bf[:wa_w], w_out_bf[wa_w:], g_ffn.reshape(1, d), wq_bf, k1_bf, k2_bf)


_CAND = [(i, j) for i in range(PEER_TOPK) for j in range(PEER_TOPK) if (i + 1) * (j + 1) <= PEER_TOPK]
_CAND_ROWS = -(-len(_CAND) // SUBLANES) * SUBLANES


def _top_sorted(s, k):
    cur = s
    vals = []
    for _ in range(k):
        mk = jnp.max(cur, axis=0, keepdims=True)
        vals.append(mk)
        cur = jnp.where(cur == mk, NEG_INF, cur)
    return vals


def _peer_select_kernel(s1_ref, s2_ref, ci_ref, cj_ref, a1_ref, a2_ref, tau_ref):
    s1 = s1_ref[...]
    s2 = s2_ref[...]
    t = s1.shape[1]
    v1 = _top_sorted(s1, PEER_TOPK)
    v2 = _top_sorted(s2, PEER_TOPK)
    m = v1[0] + v2[0]
    reps = t // LANES
    ci = jnp.concatenate([ci_ref[...]] * reps, axis=1) if reps > 1 else ci_ref[...]
    cj = jnp.concatenate([cj_ref[...]] * reps, axis=1) if reps > 1 else cj_ref[...]
    b1 = jnp.full((_CAND_ROWS, t), NEG_INF, F32)
    b2 = jnp.zeros((_CAND_ROWS, t), F32)
    for r in range(PEER_TOPK):
        b1 = jnp.where(ci == r, (v1[r] - m) * LOG2E, b1)
        b2 = jnp.where(cj == r, v2[r] * LOG2E, b2)
    top = _top_sorted(b1 + b2, PEER_TOPK)
    zsum = jnp.exp2(top[0])
    for r in range(1, PEER_TOPK):
        zsum = zsum + jnp.exp2(top[r])
    lgz = jnp.log2(zsum)
    tau_ref[...] = _top_sorted((b1 - lgz) + b2, PEER_TOPK)[PEER_TOPK - 1]
    a1_ref[...] = (s1 - m) * LOG2E - lgz
    a2_ref[...] = s2 * LOG2E


def _peer_select(s1, s2, ts):
    heads, nk, n = s1.shape
    ci = np.full((_CAND_ROWS, LANES), -1, np.int32)
    cj = np.full((_CAND_ROWS, LANES), -1, np.int32)
    for r, (i, j) in enumerate(_CAND):
        ci[r, :] = i
        cj[r, :] = j
    tab = pl.BlockSpec((None, nk, ts), lambda t, h: (h, 0, t))
    return pl.pallas_call(
        _peer_select_kernel,
        grid=(n // ts, heads),
        in_specs=[tab, tab, _full_spec(ci.shape), _full_spec(cj.shape)],
        out_specs=[tab, tab, pl.BlockSpec((None, 1, ts), lambda t, h: (h, 0, t))],
        out_shape=[jax.ShapeDtypeStruct((heads, nk, n), F32), jax.ShapeDtypeStruct((heads, nk, n), F32),
                   jax.ShapeDtypeStruct((heads, 1, n), F32)],
        compiler_params=_cparams(("parallel", "parallel")),
        name="peer_select",
    )(s1, s2, jnp.asarray(ci), jnp.asarray(cj))


def _peer_dense_kernel(xn_ref, x1_ref, a1_ref, a2_ref, tau_ref, u_ref, vt_ref, x2_ref,
                       acc_ref, act_ref, h_ref, *, heads, nk, ec):
    c = pl.program_id(1)
    tb = xn_ref.shape[0]

    @pl.when(c == 0)
    def _():
        acc_ref[...] = jnp.zeros_like(acc_ref)

    act_ref[...] = _dot_nt(u_ref[...], xn_ref[...])
    n_e1 = ec // nk
    base = pl.multiple_of(c * n_e1, SUBLANES)

    def lane_tile(lt, _):
        ls = pl.ds(pl.multiple_of(lt * LANES, LANES), LANES)
        a1 = [a1_ref[h, pl.ds(base, n_e1), ls] for h in range(heads)]
        tau = [tau_ref[h, :, ls] for h in range(heads)]
        for r in range(n_e1):
            g = jnp.zeros((nk, LANES), F32)
            for h in range(heads):
                t = a2_ref[h, :, ls] + a1[h][r:r + 1, :]
                g = g + jnp.where(t >= tau[h], jnp.exp2(t), 0.0)
            act = act_ref[r * nk:(r + 1) * nk, ls]
            h_ref[r * nk:(r + 1) * nk, ls] = (g * _gelu_tanh(act)).astype(BF16)
        return 0

    lax.fori_loop(0, tb // LANES, lane_tile, 0)
    acc_ref[...] += _dot(vt_ref[...], h_ref[...])

    @pl.when(c == pl.num_programs(1) - 1)
    def _():
        x2_ref[...] = x1_ref[...] + acc_ref[...].T


def _peer_dense(xn_bf, x1, a1, a2, tau, u_bf, vt_bf, tb, ec):
    n, d = x1.shape
    heads, nk, _ = a1.shape
    n_exp = u_bf.shape[0]
    assert (ec // nk) % SUBLANES == 0 and n_exp % ec == 0
    kern = functools.partial(_peer_dense_kernel, heads=heads, nk=nk, ec=ec)
    tab = pl.BlockSpec((heads, nk, tb), lambda i, c: (0, 0, i))
    return pl.pallas_call(
        kern,
        grid=(n // tb, n_exp // ec),
        in_specs=[pl.BlockSpec((tb, d), lambda i, c: (i, 0)),
                  pl.BlockSpec((tb, d), lambda i, c: (i, 0)),
                  tab, tab,
                  pl.BlockSpec((heads, 1, tb), lambda i, c: (0, 0, i)),
                  pl.BlockSpec((ec, d), lambda i, c: (c, 0)),
                  pl.BlockSpec((d, ec), lambda i, c: (0, c))],
        out_specs=pl.BlockSpec((tb, d), lambda i, c: (i, 0)),
        out_shape=jax.ShapeDtypeStruct((n, d), F32),
        scratch_shapes=[pltpu.VMEM((d, tb), F32), pltpu.VMEM((ec, tb), F32), pltpu.VMEM((ec, tb), BF16)],
        compiler_params=_cparams(("parallel", "arbitrary")),
        name="peer_dense",
    )(xn_bf, x1, a1, a2, tau, u_bf, vt_bf)


def _ple_final_kernel(x_ref, p_ref, gp_ref, wg_ref, wp_ref, gf_ref, y_ref):
    x = x_ref[...]
    gate = jax.nn.sigmoid(_dot(_rms(x, gp_ref[...]).astype(BF16), wg_ref[...]))
    x3 = x + gate * _dot(p_ref[...].astype(BF16), wp_ref[...])
    y_ref[...] = _rms(x3, gf_ref[...])


def _ple_final(x2, p2d, g_ple, wg_bf, wp_bf, g_final, tm):
    n, d = x2.shape
    pd = p2d.shape[1]
    return pl.pallas_call(
        _ple_final_kernel,
        grid=(n // tm,),
        in_specs=[pl.BlockSpec((tm, d), lambda i: (i, 0)),
                  pl.BlockSpec((tm, pd), lambda i: (i, 0)),
                  _full_spec((1, d)), _full_spec((d, d)), _full_spec((pd, d)), _full_spec((1, d))],
        out_specs=pl.BlockSpec((tm, d), lambda i: (i, 0)),
        out_shape=jax.ShapeDtypeStruct((n, d), F32),
        compiler_params=_cparams(("parallel",)),
        name="ple_final",
    )(x2, p2d, g_ple.reshape(1, d), wg_bf, wp_bf, g_final.reshape(1, d))


def _pick(n, pref):
    t = min(n, pref)
    while n % t:
        t //= 2
    return t


def _tiles(n_tokens, seq):
    return dict(
        tm=_pick(n_tokens, 512),
        tq=_pick(seq, 256),
        ts_lru=_pick(seq, 256),
        ts_sel=_pick(n_tokens, 512),
        tb=_pick(n_tokens, 512),
        ec=1024,
    )


def _tail(x1, xn_bf, s1, s2, p2d, wts, tiles):
    a1, a2, tau = _peer_select(s1, s2, tiles["ts_sel"])
    x2 = _peer_dense(xn_bf, x1, a1, a2, tau, wts["u"], wts["vt"], tiles["tb"], tiles["ec"])
    return _ple_final(x2, p2d, wts["g_ple"], wts["wg"], wts["wp"], wts["g_final"], tiles["tm"])


def kernel(x_prompt, x_sample, cache_k, cache_v, state_conv, state_h, page_table, p_prompt, p_sample, norm_mix, w_in, sb_bias, conv_w, conv_b, gate_a_w, gate_a_b, gate_x_w, gate_x_b, lru_lambda, norm_attn_out, norm_lru_out, w_out, norm_ffn, peer_w_q, peer_keys1, peer_keys2, peer_u, peer_v, norm_ple, ple_w_gate, ple_w_proj, norm_final):
    depth = w_in.shape[0]
    assert depth == 1, "single-layer trunk"
    bp, seq, d = x_prompt.shape
    bd, dseq, _ = x_sample.shape
    assert dseq == 1, "one new token per sampled sequence"
    heads = sb_bias.shape[1]
    head_dim = cache_k.shape[-1]
    sbw = heads * head_dim
    lruw = lru_lambda.shape[1]
    peer_heads = peer_w_q.shape[2] // (2 * peer_keys1.shape[2])

    li = 0
    w_in_bf = w_in[li].astype(BF16)
    lw = _lru_weights(conv_w[li], conv_b[li], gate_a_w[li], gate_a_b[li], gate_x_w[li], gate_x_b[li],
                      lru_lambda[li])
    w_out_bf = w_out[li].astype(BF16)
    wq_bf = peer_w_q[li].astype(BF16)
    k1_bf = peer_keys1[li].astype(BF16)
    k2_bf = peer_keys2[li].astype(BF16)
    wts = dict(u=peer_u[li].astype(BF16), vt=peer_v[li].T.astype(BF16), g_ple=norm_ple[li],
               wg=ple_w_gate[li].astype(BF16), wp=ple_w_proj[li].astype(BF16), g_final=norm_final)

    n_p = bp * seq
    tp = _tiles(n_p, seq)
    xp2d = x_prompt.reshape(n_p, d)
    q, k, v, xl, gl = _in_proj(xp2d, norm_mix[li], w_in_bf, tp["tm"])
    o_att = _sb_prompt(q, k, v, sb_bias[li], bp, seq, head_dim, tp["tq"])
    conv0 = jnp.zeros((bp, CONV_WIDTH - 1, lruw), F32)
    h0 = jnp.zeros((bp, lruw), F32)
    o_lru, conv_p, h_p = _lru_prompt(xl, gl, conv0, h0, lw, bp, seq, tp["ts_lru"])
    x1, xn_bf, s1, s2 = _mix_peer_q(o_att, o_lru, xp2d, norm_attn_out[li], norm_lru_out[li], w_out_bf,
                                    norm_ffn[li], wq_bf, k1_bf, k2_bf, peer_heads, tp["tm"])
    y_p = _tail(x1, xn_bf, s1, s2, p_prompt[li].reshape(n_p, -1), wts, tp)

    ts_ = _tiles(bd, 1)
    xs2d = x_sample.reshape(bd, d)
    qs, ks, vs, xls, gls = _in_proj(xs2d, norm_mix[li], w_in_bf, ts_["tm"])
    o_att_s = _sb_sample(qs, sb_bias[li], cache_k[li], cache_v[li], page_table, heads, head_dim, pp=8)
    o_lru_s, h_s = _lru_sample(xls, gls, state_conv[li], state_h[li], lw)
    conv_s = jnp.concatenate([state_conv[li][:, 1:], xls[:, None, :]], axis=1)
    x1s, xns_bf, s1s, s2s = _mix_peer_q(o_att_s, o_lru_s, xs2d, norm_attn_out[li], norm_lru_out[li], w_out_bf,
                                        norm_ffn[li], wq_bf, k1_bf, k2_bf, peer_heads, ts_["tm"])
    y_s = _tail(x1s, xns_bf, s1s, s2s, p_sample[li].reshape(bd, -1), wts, ts_)

    return (y_p.reshape(bp, seq, d), y_s.reshape(bd, 1, d),
            k.reshape(1, bp, seq, heads, head_dim), v.reshape(1, bp, seq, heads, head_dim),
            conv_p[None], h_p.reshape(1, bp, lruw),
            ks.reshape(1, bd, 1, heads, head_dim), vs.reshape(1, bd, 1, heads, head_dim),
            conv_s[None], h_s[None])
```

```python
import functools
import math

import numpy as np
import jax
import jax.numpy as jnp
from jax import lax
from jax.experimental import pallas as pl
from jax.experimental.pallas import tpu as pltpu

F32 = jnp.float32
BF16 = jnp.bfloat16

RMS_EPS = 1e-6
LRU_C = 8.0
CONV_WIDTH = 4
PEER_TOPK = 16
LOG2E = 1.4426950408889634
NEG_INF = float("-inf")

LANES = 128
SUBLANES = 8
VMEM_LIMIT = 56 * 1024 * 1024


def _cparams(sem, flags=None):
    return pltpu.CompilerParams(dimension_semantics=sem, vmem_limit_bytes=VMEM_LIMIT, flags=flags)


def _rms(x, g):
    ms = jnp.mean(x * x, axis=-1, keepdims=True)
    return x * lax.rsqrt(ms + RMS_EPS) * g


def _softplus(z):
    return jnp.maximum(z, 0.0) + jnp.log1p(jnp.exp(-jnp.abs(z)))


def _log2_keep(z2):
    nz = -z2
    return jnp.minimum(nz, 0.0) - jnp.log2(1.0 + jnp.exp2(jnp.minimum(z2, nz)))


def _gelu_tanh(x):
    c = math.sqrt(2.0 / math.pi)
    return 0.5 * x * (1.0 + jnp.tanh(c * (x + 0.044715 * (x * x * x))))


def _dot(a, b):
    return jnp.dot(a, b, preferred_element_type=F32)


def _dot_nt(a, b):
    return lax.dot_general(a, b, (((1,), (1,)), ((), ())), preferred_element_type=F32)


def _split_bf16(x):
    hi = x.astype(BF16)
    lo = (x - hi.astype(F32)).astype(BF16)
    return hi, lo


def _full_spec(shape):
    nd = len(shape)
    return pl.BlockSpec(shape, lambda *_: (0,) * nd)


def _in_proj_kernel(x_ref, g_ref, w_ref, q_ref, k_ref, v_ref, xl_ref, gl_ref):
    xn = _rms(x_ref[...], g_ref[...]).astype(BF16)
    proj = _dot(xn, w_ref[...])
    width = q_ref.shape[-1]
    for i, ref in enumerate((q_ref, k_ref, v_ref, xl_ref, gl_ref)):
        ref[...] = proj[:, i * width:(i + 1) * width]


def _in_proj(x2d, norm_g, w_in_bf, tm):
    n, d = x2d.shape
    width = w_in_bf.shape[1] // 5
    out = jax.ShapeDtypeStruct((n, width), F32)
    return pl.pallas_call(
        _in_proj_kernel,
        grid=(n // tm,),
        in_specs=[pl.BlockSpec((tm, d), lambda i: (i, 0)),
                  _full_spec((1, d)),
                  _full_spec(w_in_bf.shape)],
        out_specs=[pl.BlockSpec((tm, width), lambda i: (i, 0))] * 5,
        out_shape=[out] * 5,
        compiler_params=_cparams(("parallel",)),
        name="in_proj",
    )(x2d, norm_g.reshape(1, d), w_in_bf)


def _sb_prompt_kernel(bias_ref, q_ref, k_ref, v_ref, tri_ref, o_ref, *, tq, head_dim):
    hp = pl.program_id(1)
    i = pl.program_id(2)
    nh = LANES // head_dim
    lane = lax.broadcasted_iota(jnp.int32, (1, LANES), 1)
    q = q_ref[...] * (head_dim ** -0.5 * LOG2E)
    tri = tri_ref[...]
    row = lax.broadcasted_iota(jnp.int32, (tq, tq), 0)
    col = lax.broadcasted_iota(jnp.int32, (tq, tq), 1)
    diag_valid = col < row
    in_head = [(lane >= hh * head_dim) & (lane < (hh + 1) * head_dim) for hh in range(nh)]
    bias = [bias_ref[hp * nh + hh] * LOG2E for hh in range(nh)]
    qh = [jnp.where(in_head[hh], q, 0.0).astype(BF16) for hh in range(nh)]

    def process(blocks, state):
        staged = []
        for j, masked in blocks:
            start = pl.multiple_of(j * tq, tq)
            kblk = k_ref[pl.ds(start, tq), :].astype(BF16)
            vblk = v_ref[pl.ds(start, tq), :].astype(BF16)
            per_head = []
            for hh in range(nh):
                z = _dot_nt(qh[hh], kblk) + bias[hh]
                lk = _log2_keep(z)
                if masked:
                    lk = jnp.where(diag_valid, lk, 0.0)
                hi, lo = _split_bf16(lk)
                r_incl = _dot(hi, tri) + _dot(lo, tri)
                per_head.append((z, r_incl, jnp.sum(lk, axis=1, keepdims=True)))
            staged.append((vblk, masked, per_head))
        out_state = []
        for hh in range(nh):
            o_acc, carry = state[2 * hh], state[2 * hh + 1]
            for vblk, masked, per_head in staged:
                z, r_incl, tot = per_head[hh]
                w = jnp.exp2(z + r_incl + carry)
                if masked:
                    w = jnp.where(diag_valid, w, 0.0)
                o_acc = o_acc + _dot(w.astype(BF16), vblk)
                carry = carry + tot
            out_state += [o_acc, carry]
        return tuple(out_state)

    state = (jnp.zeros((tq, LANES), F32), jnp.zeros((tq, 1), F32)) * nh
    state = process([(i, True)], state)
    state = lax.fori_loop(
        0, i // 2, lambda jj, st: process([(i - 1 - 2 * jj, False), (i - 2 - 2 * jj, False)], st), state)
    state = lax.fori_loop(0, i % 2, lambda jj, st: process([(0, False)], st), state)
    out = jnp.zeros((tq, LANES), F32)
    for hh in range(nh):
        out = jnp.where(in_head[hh], state[2 * hh], out)
    o_ref[...] = out


def _sb_prompt(q, k, v, sb_bias, batch, seq, head_dim, tq):
    n, width = q.shape
    nblk = seq // tq
    npair = width // LANES
    tri = jnp.asarray(np.tril(np.ones((tq, tq), np.float32)), BF16)
    kern = functools.partial(_sb_prompt_kernel, tq=tq, head_dim=head_dim)
    return pl.pallas_call(
        kern,
        grid=(batch, npair, nblk),
        in_specs=[pl.BlockSpec(memory_space=pltpu.SMEM),
                  pl.BlockSpec((tq, LANES), lambda b, h, i: (b * nblk + i, h)),
                  pl.BlockSpec((seq, LANES), lambda b, h, i: (b, h)),
                  pl.BlockSpec((seq, LANES), lambda b, h, i: (b, h)),
                  _full_spec((tq, tq))],
        out_specs=pl.BlockSpec((tq, LANES), lambda b, h, i: (b * nblk + i, h)),
        out_shape=jax.ShapeDtypeStruct((n, width), F32),
        compiler_params=_cparams(("parallel", "parallel", "arbitrary")),
        name="sb_prompt",
    )(sb_bias, q, k, v, tri)


def _sb_sample_kernel(pt_ref, bias_ref, qcol_ref, tri_ref, *refs, pp, heads, head_dim):
    k_refs = refs[:pp]
    v_refs = refs[pp:2 * pp]
    o_ref = refs[2 * pp]
    acc_ref, carry_ref = refs[2 * pp + 1:]
    j = pl.program_id(1)
    page = tri_ref.shape[0]
    assert heads == SUBLANES and head_dim % SUBLANES == 0
    nslab = head_dim // SUBLANES

    @pl.when(j == 0)
    def _():
        acc_ref[...] = jnp.zeros_like(acc_ref)
        carry_ref[...] = jnp.zeros_like(carry_ref)

    def slab(ref, h, r):
        lo = h * head_dim + r * SUBLANES
        return ref[lo:lo + SUBLANES, :]

    sub = lax.broadcasted_iota(jnp.int32, (SUBLANES, page), 0)
    zs = [jnp.zeros((SUBLANES, page), F32) for _ in range(pp)]
    for h in range(heads):
        qs = [slab(qcol_ref, h, r) for r in range(nslab)]
        for p in range(pp):
            s8 = slab(k_refs[p], h, 0) * qs[0]
            for r in range(1, nslab):
                s8 = s8 + slab(k_refs[p], h, r) * qs[r]
            for sh in (4, 2, 1):
                s8 = s8 + pltpu.roll(s8, sh, 0)
            zs[p] = jnp.where(sub == h, s8, zs[p])
    z = jnp.concatenate(zs, axis=0) + jnp.concatenate([bias_ref[...] * LOG2E] * pp, axis=0)
    lk = _log2_keep(z)
    hi, lo = _split_bf16(lk)
    tri = tri_ref[...]
    r_incl = _dot(hi, tri) + _dot(lo, tri)
    tot = jnp.sum(lk, axis=1, keepdims=True)
    carry = carry_ref[...]
    ws = []
    for p in range(pp):
        sl = slice(p * heads, (p + 1) * heads)
        ws.append(jnp.exp2(z[sl] + r_incl[sl] + carry))
        carry = carry + tot[sl]
    carry_ref[...] = carry
    for h in range(heads):
        acc = [slab(acc_ref, h, r) for r in range(nslab)]
        for p in range(pp):
            wb = jnp.broadcast_to(ws[p][h:h + 1, :], (SUBLANES, page))
            for r in range(nslab):
                acc[r] = acc[r] + slab(v_refs[p], h, r) * wb
        for r in range(nslab):
            lo_r = h * head_dim + r * SUBLANES
            acc_ref[lo_r:lo_r + SUBLANES, :] = acc[r]

    @pl.when(j == pl.num_programs(1) - 1)
    def _():
        o_ref[...] = jnp.sum(acc_ref[...], axis=1, keepdims=True)


def _sb_sample(q, sb_bias, cache_k, cache_v, page_table, heads, head_dim, pp):
    bd, width = q.shape
    n_phys, page = cache_k.shape[0], cache_k.shape[1]
    n_pages = page_table.shape[1]
    assert n_pages % pp == 0
    ck = jnp.transpose(cache_k, (0, 2, 3, 1)).reshape(n_phys, width, page)
    cv = jnp.transpose(cache_v, (0, 2, 3, 1)).reshape(n_phys, width, page)
    qcol = jnp.broadcast_to((q * (head_dim ** -0.5 * LOG2E))[:, :, None], (bd, width, page))
    tri = jnp.asarray(np.tril(np.ones((page, page), np.float32)), BF16)
    kern = functools.partial(_sb_sample_kernel, pp=pp, heads=heads, head_dim=head_dim)

    def page_map(p):
        return lambda b, j, pt: (pt[b, n_pages - 1 - (j * pp + p)], 0, 0)

    kv_specs = [pl.BlockSpec((None, width, page), page_map(p)) for p in range(pp)]
    grid_spec = pltpu.PrefetchScalarGridSpec(
        num_scalar_prefetch=1,
        grid=(bd, n_pages // pp),
        in_specs=[pl.BlockSpec((heads, 1), lambda b, j, pt: (0, 0)),
                  pl.BlockSpec((None, width, page), lambda b, j, pt: (b, 0, 0)),
                  pl.BlockSpec((page, page), lambda b, j, pt: (0, 0))] + kv_specs + kv_specs,
        out_specs=pl.BlockSpec((None, width, 1), lambda b, j, pt: (b, 0, 0)),
        scratch_shapes=[pltpu.VMEM((width, page), F32),
                        pltpu.VMEM((heads, 1), F32)],
    )
    out = pl.pallas_call(
        kern,
        grid_spec=grid_spec,
        out_shape=jax.ShapeDtypeStruct((bd, width, 1), F32),
        compiler_params=_cparams(("parallel", "arbitrary")),
        name="sb_sample",
    )(page_table, sb_bias.reshape(heads, 1), qcol, tri, *([ck] * pp), *([cv] * pp))
    return out.reshape(bd, width)


def _lru_gates(xc, wa_ref, ba_ref, wx_ref, bx_ref, lam_ref):
    xcb = xc.astype(BF16)
    r = jax.nn.sigmoid(_dot(xcb, wa_ref[...]) + ba_ref[...])
    ig = jax.nn.sigmoid(_dot(xcb, wx_ref[...]) + bx_ref[...])
    log_a = (-LRU_C) * r * _softplus(-lam_ref[...])
    a = jnp.exp(log_a)
    one_minus_a2 = -jnp.tanh(log_a) * (a * a + 1.0)
    u = jnp.sqrt(one_minus_a2) * (ig * xc)
    return a, u


def _lru_prompt_kernel(xl_ref, gl_ref, conv0_ref, h0_ref, cw_ref, cb_ref, wa_ref, ba_ref, wx_ref, bx_ref,
                       lam_ref, o_ref, convn_ref, hn_ref, xpad_ref, h_ref, *, ts):
    tb = pl.program_id(1)
    tail = CONV_WIDTH - 1

    @pl.when(tb == 0)
    def _():
        xpad_ref[SUBLANES - tail:SUBLANES, :] = conv0_ref[...]
        h_ref[...] = h0_ref[...]

    xl = xl_ref[...]
    xpad_ref[SUBLANES:SUBLANES + ts, :] = xl
    cw = cw_ref[...]
    xc = cb_ref[...]
    for jw in range(CONV_WIDTH):
        off = SUBLANES - tail + jw
        xc = xc + xpad_ref[off:off + ts, :] * cw[jw:jw + 1, :]
    a, u = _lru_gates(xc, wa_ref, ba_ref, wx_ref, bx_ref, lam_ref)
    rows = lax.broadcasted_iota(jnp.int32, a.shape, 0)
    d = 1
    while d < ts:
        keep = rows >= d
        a_sh = jnp.where(keep, pltpu.roll(a, d, 0), 1.0)
        u_sh = jnp.where(keep, pltpu.roll(u, d, 0), 0.0)
        u = u + a * u_sh
        a = a * a_sh
        d *= 2
    hs = a * h_ref[...] + u
    o_ref[...] = hs * _gelu_tanh(gl_ref[...])
    h_ref[...] = hs[ts - 1:ts, :]
    xpad_ref[SUBLANES - tail:SUBLANES, :] = xl[ts - tail:ts, :]

    @pl.when(tb == pl.num_programs(1) - 1)
    def _():
        convn_ref[...] = xl[ts - tail:ts, :]
        hn_ref[...] = hs[ts - 1:ts, :]


def _lru_weights(conv_w, conv_b, gate_a_w, gate_a_b, gate_x_w, gate_x_b, lru_lambda):
    nb, bd, _ = gate_a_w.shape
    w = nb * bd
    eye = jnp.eye(nb, dtype=F32)
    wa = jnp.einsum('nde,nm->ndme', gate_a_w, eye).reshape(w, w).astype(BF16)
    wx = jnp.einsum('nde,nm->ndme', gate_x_w, eye).reshape(w, w).astype(BF16)
    return (conv_w, conv_b.reshape(1, w), wa, gate_a_b.reshape(1, w), wx, gate_x_b.reshape(1, w),
            lru_lambda.reshape(1, w))


def _lru_prompt(xl, gl, conv0, h0, lw, batch, seq, ts):
    n, w = xl.shape
    nblk = seq // ts
    tail = CONV_WIDTH - 1
    kern = functools.partial(_lru_prompt_kernel, ts=ts)
    tok = pl.BlockSpec((ts, w), lambda b, t: (b * nblk + t, 0))
    return pl.pallas_call(
        kern,
        grid=(batch, nblk),
        in_specs=[tok, tok,
                  pl.BlockSpec((None, tail, w), lambda b, t: (b, 0, 0)),
                  pl.BlockSpec((None, 1, w), lambda b, t: (b, 0, 0)),
                  _full_spec((CONV_WIDTH, w)), _full_spec((1, w)),
                  _full_spec((w, w)), _full_spec((1, w)),
                  _full_spec((w, w)), _full_spec((1, w)), _full_spec((1, w))],
        out_specs=[tok,
                   pl.BlockSpec((None, tail, w), lambda b, t: (b, 0, 0)),
                   pl.BlockSpec((None, 1, w), lambda b, t: (b, 0, 0))],
        out_shape=[jax.ShapeDtypeStruct((n, w), F32),
                   jax.ShapeDtypeStruct((batch, tail, w), F32),
                   jax.ShapeDtypeStruct((batch, 1, w), F32)],
        scratch_shapes=[pltpu.VMEM((SUBLANES + ts, w), F32), pltpu.VMEM((1, w), F32)],
        compiler_params=_cparams(("parallel", "arbitrary")),
        name="lru_prompt",
    )(xl, gl, conv0, h0.reshape(batch, 1, w), *lw)


def _lru_sample_kernel(xl_ref, gl_ref, c0_ref, c1_ref, c2_ref, h0_ref, cw_ref, cb_ref, wa_ref, ba_ref,
                       wx_ref, bx_ref, lam_ref, o_ref, hn_ref):
    cw = cw_ref[...]
    xl = xl_ref[...]
    xc = cb_ref[...] + c0_ref[...] * cw[0:1, :]
    xc = xc + c1_ref[...] * cw[1:2, :]
    xc = xc + c2_ref[...] * cw[2:3, :]
    xc = xc + xl * cw[3:4, :]
    a, u = _lru_gates(xc, wa_ref, ba_ref, wx_ref, bx_ref, lam_ref)
    h = a * h0_ref[...] + u
    hn_ref[...] = h
    o_ref[...] = h * _gelu_tanh(gl_ref[...])


def _lru_sample(xl, gl, state_conv, h0, lw):
    bd, w = xl.shape
    full = _full_spec((bd, w))
    return pl.pallas_call(
        _lru_sample_kernel,
        grid=(1,),
        in_specs=[full] * 6 + [_full_spec((CONV_WIDTH, w)), _full_spec((1, w)),
                               _full_spec((w, w)), _full_spec((1, w)),
                               _full_spec((w, w)), _full_spec((1, w)), _full_spec((1, w))],
        out_specs=[full, full],
        out_shape=[jax.ShapeDtypeStruct((bd, w), F32)] * 2,
        compiler_params=_cparams(("arbitrary",)),
        name="lru_sample",
    )(xl, gl, state_conv[:, 0], state_conv[:, 1], state_conv[:, 2], h0, *lw)


def _mix_peer_q_kernel(oa_ref, ol_ref, x_ref, ga_ref, gl_ref, wa_ref, wl_ref, gf_ref, wq_ref, k1_ref, k2_ref,
                       x1_ref, xn_ref, s1_ref, s2_ref, *, heads, half):
    na = _rms(oa_ref[...], ga_ref[...]).astype(BF16)
    nl = _rms(ol_ref[...], gl_ref[...]).astype(BF16)
    x1 = x_ref[...] + (_dot(na, wa_ref[...]) + _dot(nl, wl_ref[...]))
    x1_ref[...] = x1
    xn = _rms(x1, gf_ref[...]).astype(BF16)
    xn_ref[...] = xn
    qp = _dot(xn, wq_ref[...]).astype(BF16)
    k1 = k1_ref[...]
    k2 = k2_ref[...]
    for h in range(heads):
        base = h * 2 * half
        s1_ref[h] = _dot_nt(k1, qp[:, base:base + half])
        s2_ref[h] = _dot_nt(k2, qp[:, base + half:base + 2 * half])


def _mix_peer_q(o_att, o_lru, x2d, g_att, g_lru, w_out_bf, g_ffn, wq_bf, k1_bf, k2_bf, heads, tm):
    n, d = x2d.shape
    wa_w = o_att.shape[1]
    nk, half = k1_bf.shape
    kern = functools.partial(_mix_peer_q_kernel, heads=heads, half=half)
    tok = lambda w: pl.BlockSpec((tm, w), lambda i: (i, 0))
    sspec = pl.BlockSpec((heads, nk, tm), lambda i: (0, 0, i))
    return pl.pallas_call(
        kern,
        grid=(n // tm,),
        in_specs=[tok(wa_w), tok(o_lru.shape[1]), tok(d),
                  _full_spec((1, wa_w)), _full_spec((1, o_lru.shape[1])),
                  _full_spec((wa_w, d)), _full_spec((o_lru.shape[1], d)),
                  _full_spec((1, d)), _full_spec(wq_bf.shape),
                  _full_spec(k1_bf.shape), _full_spec(k2_bf.shape)],
        out_specs=[tok(d), tok(d), sspec, sspec],
        out_shape=[jax.ShapeDtypeStruct((n, d), F32), jax.ShapeDtypeStruct((n, d), BF16),
                   jax.ShapeDtypeStruct((heads, nk, n), F32), jax.ShapeDtypeStruct((heads, nk, n), F32)],
        compiler_params=_cparams(("parallel",)),
        name="mix_peer_q",
    )(o_att, o_lru, x2d, g_att.reshape(1, -1), g_lru.reshape(1, -1),
      w_out_bf[:wa_w], w_out_bf[wa_w:], g_ffn.reshape(1, d), wq_bf, k1_bf, k2_bf)


_CAND = [(i, j) for i in range(PEER_TOPK) for j in range(PEER_TOPK) if (i + 1) * (j + 1) <= PEER_TOPK]
_CAND_ROWS = -(-len(_CAND) // SUBLANES) * SUBLANES


def _top_sorted(s, k):
    cur = s
    vals = []
    for _ in range(k):
        mk = jnp.max(cur, axis=0, keepdims=True)
        vals.append(mk)
        cur = jnp.where(cur == mk, NEG_INF, cur)
    return vals


def _peer_select_kernel(s1_ref, s2_ref, ci_ref, cj_ref, a1_ref, a2_ref, tau_ref):
    s1 = s1_ref[...]
    s2 = s2_ref[...]
    t = s1.shape[1]
    v1 = _top_sorted(s1, PEER_TOPK)
    v2 = _top_sorted(s2, PEER_TOPK)
    reps = t // LANES
    ci = jnp.concatenate([ci_ref[...]] * reps, axis=1) if reps > 1 else ci_ref[...]
    cj = jnp.concatenate([cj_ref[...]] * reps, axis=1) if reps > 1 else cj_ref[...]
    b1 = jnp.zeros((_CAND_ROWS, t), F32)
    b2 = jnp.zeros((_CAND_ROWS, t), F32)
    for r in range(PEER_TOPK):
        b1 = jnp.where(ci == r, jnp.exp2((v1[r] - v1[0]) * LOG2E), b1)
        b2 = jnp.where(cj == r, jnp.exp2((v2[r] - v2[0]) * LOG2E), b2)
    top = _top_sorted(b1 * b2, PEER_TOPK)
    zsum = top[0]
    for r in range(1, PEER_TOPK):
        zsum = zsum + top[r]
    scale = 0.5 / zsum
    tau_ref[...] = _top_sorted((b1 * scale) * b2, PEER_TOPK)[PEER_TOPK - 1]
    a1_ref[...] = jnp.exp2((s1 - v1[0]) * LOG2E) * scale
    a2_ref[...] = jnp.exp2((s2 - v2[0]) * LOG2E)


def _peer_select(s1, s2, ts):
    heads, nk, n = s1.shape
    ci = np.full((_CAND_ROWS, LANES), -1, np.int32)
    cj = np.full((_CAND_ROWS, LANES), -1, np.int32)
    for r, (i, j) in enumerate(_CAND):
        ci[r, :] = i
        cj[r, :] = j
    tab = pl.BlockSpec((None, nk, ts), lambda t, h: (h, 0, t))
    return pl.pallas_call(
        _peer_select_kernel,
        grid=(n // ts, heads),
        in_specs=[tab, tab, _full_spec(ci.shape), _full_spec(cj.shape)],
        out_specs=[tab, tab, pl.BlockSpec((None, 1, ts), lambda t, h: (h, 0, t))],
        out_shape=[jax.ShapeDtypeStruct((heads, nk, n), F32), jax.ShapeDtypeStruct((heads, nk, n), F32),
                   jax.ShapeDtypeStruct((heads, 1, n), F32)],
        compiler_params=_cparams(("parallel", "parallel")),
        name="peer_select",
    )(s1, s2, jnp.asarray(ci), jnp.asarray(cj))


def _peer_dense_kernel(xn_ref, x1_ref, a1_ref, a2_ref, tau_ref, u_ref, vt_ref, x2_ref,
                       acc_ref, act_ref, h_ref, *, heads, nk, ec):
    c = pl.program_id(1)
    tb = xn_ref.shape[0]
    n_e1 = ec // nk
    c_in = math.sqrt(2.0 / math.pi)

    @pl.when(c == 0)
    def _():
        acc_ref[...] = jnp.zeros_like(acc_ref)

    act_ref[...] = _dot_nt(u_ref[...], xn_ref[...])

    def lane_tile(lt, _):
        ls = pl.ds(pl.multiple_of(lt * LANES, LANES), LANES)
        tau = [jnp.broadcast_to(tau_ref[h, :, ls], (SUBLANES, LANES)) for h in range(heads)]
        for r in range(n_e1):
            row = [jnp.broadcast_to(a1_ref[h, r:r + 1, ls], (SUBLANES, LANES)) for h in range(heads)]
            for sb in range(0, nk, 2 * SUBLANES):
                halves = []
                for s0 in (sb, sb + SUBLANES):
                    gate = None
                    for h in range(heads):
                        e = a2_ref[h, s0:s0 + SUBLANES, ls] * row[h]
                        kept = jnp.where(e >= tau[h], e, 0.0)
                        gate = kept if gate is None else gate + kept
                    x = act_ref[r * nk + s0:r * nk + s0 + SUBLANES, ls]
                    inner = x * ((x * x) * (0.044715 * c_in) + c_in)
                    halves.append(gate * (x * (1.0 + jnp.tanh(inner))))
                h_ref[r * nk + sb:r * nk + sb + 2 * SUBLANES, ls] = jnp.concatenate(halves, axis=0).astype(BF16)
        return 0

    lax.fori_loop(0, tb // LANES, lane_tile, 0)
    acc_ref[...] += _dot(vt_ref[...], h_ref[...])

    @pl.when(c == pl.num_programs(1) - 1)
    def _():
        x2_ref[...] = x1_ref[...] + acc_ref[...].T


def _peer_dense(xn_bf, x1, a1, a2, tau, u_bf, vt_bf, tb, ec):
    n, d = x1.shape
    heads, nk, _ = a1.shape
    n_exp = u_bf.shape[0]
    n_e1 = ec // nk
    assert n_e1 % SUBLANES == 0 and n_exp % ec == 0 and n % tb == 0
    kern = functools.partial(_peer_dense_kernel, heads=heads, nk=nk, ec=ec)
    return pl.pallas_call(
        kern,
        grid=(n // tb, n_exp // ec),
        in_specs=[pl.BlockSpec((tb, d), lambda i, c: (i, 0)),
                  pl.BlockSpec((tb, d), lambda i, c: (i, 0)),
                  pl.BlockSpec((heads, n_e1, tb), lambda i, c: (0, c, i)),
                  pl.BlockSpec((heads, nk, tb), lambda i, c: (0, 0, i)),
                  pl.BlockSpec((heads, 1, tb), lambda i, c: (0, 0, i)),
                  pl.BlockSpec((ec, d), lambda i, c: (c, 0)),
                  pl.BlockSpec((d, ec), lambda i, c: (0, c))],
        out_specs=pl.BlockSpec((tb, d), lambda i, c: (i, 0)),
        out_shape=jax.ShapeDtypeStruct((n, d), F32),
        scratch_shapes=[pltpu.VMEM((d, tb), F32), pltpu.VMEM((ec, tb), F32), pltpu.VMEM((ec, tb), BF16)],
        compiler_params=_cparams(("parallel", "arbitrary")),
        name="peer_dense",
    )(xn_bf, x1, a1, a2, tau, u_bf, vt_bf)


def _ple_final_kernel(x_ref, p_ref, gp_ref, wg_ref, wp_ref, gf_ref, y_ref):
    x = x_ref[...]
    gate = jax.nn.sigmoid(_dot(_rms(x, gp_ref[...]).astype(BF16), wg_ref[...]))
    x3 = x + gate * _dot(p_ref[...].astype(BF16), wp_ref[...])
    y_ref[...] = _rms(x3, gf_ref[...])


def _ple_final(x2, p2d, g_ple, wg_bf, wp_bf, g_final, tm):
    n, d = x2.shape
    pd = p2d.shape[1]
    return pl.pallas_call(
        _ple_final_kernel,
        grid=(n // tm,),
        in_specs=[pl.BlockSpec((tm, d), lambda i: (i, 0)),
                  pl.BlockSpec((tm, pd), lambda i: (i, 0)),
                  _full_spec((1, d)), _full_spec((d, d)), _full_spec((pd, d)), _full_spec((1, d))],
        out_specs=pl.BlockSpec((tm, d), lambda i: (i, 0)),
        out_shape=jax.ShapeDtypeStruct((n, d), F32),
        compiler_params=_cparams(("parallel",)),
        name="ple_final",
    )(x2, p2d, g_ple.reshape(1, d), wg_bf, wp_bf, g_final.reshape(1, d))


def _pick(n, pref):
    t = min(n, pref)
    while n % t:
        t //= 2
    return t


def _tiles(n_tokens, seq):
    return dict(
        tm=_pick(n_tokens, 512),
        tq=_pick(seq, 256),
        ts_lru=_pick(seq, 256),
        ts_sel=_pick(n_tokens, 512),
        tb=_pick(n_tokens, 512),
        ec=1024,
    )


def _tail(x1, xn_bf, s1, s2, p2d, wts, tiles):
    a1, a2, tau = _peer_select(s1, s2, tiles["ts_sel"])
    x2 = _peer_dense(xn_bf, x1, a1, a2, tau, wts["u"], wts["vt"], tiles["tb"], tiles["ec"])
    return _ple_final(x2, p2d, wts["g_ple"], wts["wg"], wts["wp"], wts["g_final"], tiles["tm"])


def kernel(x_prompt, x_sample, cache_k, cache_v, state_conv, state_h, page_table, p_prompt, p_sample, norm_mix, w_in, sb_bias, conv_w, conv_b, gate_a_w, gate_a_b, gate_x_w, gate_x_b, lru_lambda, norm_attn_out, norm_lru_out, w_out, norm_ffn, peer_w_q, peer_keys1, peer_keys2, peer_u, peer_v, norm_ple, ple_w_gate, ple_w_proj, norm_final):
    depth = w_in.shape[0]
    assert depth == 1, "single-layer trunk"
    bp, seq, d = x_prompt.shape
    bd, dseq, _ = x_sample.shape
    assert dseq == 1, "one new token per sampled sequence"
    heads = sb_bias.shape[1]
    head_dim = cache_k.shape[-1]
    sbw = heads * head_dim
    lruw = lru_lambda.shape[1]
    peer_heads = peer_w_q.shape[2] // (2 * peer_keys1.shape[2])

    li = 0
    w_in_bf = w_in[li].astype(BF16)
    lw = _lru_weights(conv_w[li], conv_b[li], gate_a_w[li], gate_a_b[li], gate_x_w[li], gate_x_b[li],
                      lru_lambda[li])
    w_out_bf = w_out[li].astype(BF16)
    wq_bf = peer_w_q[li].astype(BF16)
    k1_bf = peer_keys1[li].astype(BF16)
    k2_bf = peer_keys2[li].astype(BF16)
    wts = dict(u=peer_u[li].astype(BF16), vt=peer_v[li].T.astype(BF16), g_ple=norm_ple[li],
               wg=ple_w_gate[li].astype(BF16), wp=ple_w_proj[li].astype(BF16), g_final=norm_final)

    n_p = bp * seq
    tp = _tiles(n_p, seq)
    xp2d = x_prompt.reshape(n_p, d)
    q, k, v, xl, gl = _in_proj(xp2d, norm_mix[li], w_in_bf, tp["tm"])
    o_att = _sb_prompt(q, k, v, sb_bias[li], bp, seq, head_dim, tp["tq"])
    conv0 = jnp.zeros((bp, CONV_WIDTH - 1, lruw), F32)
    h0 = jnp.zeros((bp, lruw), F32)
    o_lru, conv_p, h_p = _lru_prompt(xl, gl, conv0, h0, lw, bp, seq, tp["ts_lru"])
    x1, xn_bf, s1, s2 = _mix_peer_q(o_att, o_lru, xp2d, norm_attn_out[li], norm_lru_out[li], w_out_bf,
                                    norm_ffn[li], wq_bf, k1_bf, k2_bf, peer_heads, tp["tm"])
    y_p = _tail(x1, xn_bf, s1, s2, p_prompt[li].reshape(n_p, -1), wts, tp)

    ts_ = _tiles(bd, 1)
    xs2d = x_sample.reshape(bd, d)
    qs, ks, vs, xls, gls = _in_proj(xs2d, norm_mix[li], w_in_bf, ts_["tm"])
    o_att_s = _sb_sample(qs, sb_bias[li], cache_k[li], cache_v[li], page_table, heads, head_dim,
                         pp=math.gcd(page_table.shape[1], 16))
    o_lru_s, h_s = _lru_sample(xls, gls, state_conv[li], state_h[li], lw)
    conv_s = jnp.concatenate([state_conv[li][:, 1:], xls[:, None, :]], axis=1)
    x1s, xns_bf, s1s, s2s = _mix_peer_q(o_att_s, o_lru_s, xs2d, norm_attn_out[li], norm_lru_out[li], w_out_bf,
                                        norm_ffn[li], wq_bf, k1_bf, k2_bf, peer_heads, ts_["tm"])
    y_s = _tail(x1s, xns_bf, s1s, s2s, p_sample[li].reshape(bd, -1), wts, ts_)

    return (y_p.reshape(bp, seq, d), y_s.reshape(bd, 1, d),
            k.reshape(1, bp, seq, heads, head_dim), v.reshape(1, bp, seq, heads, head_dim),
            conv_p[None], h_p.reshape(1, bp, lruw),
            ks.reshape(1, bd, 1, heads, head_dim), vs.reshape(1, bd, 1, heads, head_dim),
            conv_s[None], h_s[None])
```

```python
import functools
import math

import numpy as np
import jax
import jax.numpy as jnp
from jax import lax
from jax.experimental import pallas as pl
from jax.experimental.pallas import tpu as pltpu

F32 = jnp.float32
BF16 = jnp.bfloat16

RMS_EPS = 1e-6
LRU_C = 8.0
CONV_WIDTH = 4
PEER_TOPK = 16
LOG2E = 1.4426950408889634
NEG_INF = float("-inf")

LANES = 128
SUBLANES = 8
VMEM_LIMIT = 56 * 1024 * 1024


def _cparams(sem, flags=None):
    return pltpu.CompilerParams(dimension_semantics=sem, vmem_limit_bytes=VMEM_LIMIT, flags=flags)


def _rms(x, g):
    ms = jnp.mean(x * x, axis=-1, keepdims=True)
    return x * lax.rsqrt(ms + RMS_EPS) * g


def _softplus(z):
    return jnp.maximum(z, 0.0) + jnp.log1p(jnp.exp(-jnp.abs(z)))


def _log2_keep(z2):
    nz = -z2
    return jnp.minimum(nz, 0.0) - jnp.log2(1.0 + jnp.exp2(jnp.minimum(z2, nz)))


def _gelu_tanh(x):
    c = math.sqrt(2.0 / math.pi)
    return 0.5 * x * (1.0 + jnp.tanh(c * (x + 0.044715 * (x * x * x))))


def _dot(a, b):
    return jnp.dot(a, b, preferred_element_type=F32)


def _dot_nt(a, b):
    return lax.dot_general(a, b, (((1,), (1,)), ((), ())), preferred_element_type=F32)


def _split_bf16(x):
    hi = x.astype(BF16)
    lo = (x - hi.astype(F32)).astype(BF16)
    return hi, lo


def _full_spec(shape):
    nd = len(shape)
    return pl.BlockSpec(shape, lambda *_: (0,) * nd)


def _in_proj_kernel(x_ref, g_ref, w_ref, q_ref, k_ref, v_ref, xl_ref, gl_ref, *, kv_feature_major):
    xn = _rms(x_ref[...], g_ref[...]).astype(BF16)
    proj = _dot(xn, w_ref[...])
    width = q_ref.shape[-1]
    for i, ref in enumerate((q_ref, k_ref, v_ref, xl_ref, gl_ref)):
        part = proj[:, i * width:(i + 1) * width]
        ref[...] = part.T if (kv_feature_major and i in (1, 2)) else part


def _in_proj(x2d, norm_g, w_in_bf, tm, seq=None):
    n, d = x2d.shape
    width = w_in_bf.shape[1] // 5
    out = jax.ShapeDtypeStruct((n, width), F32)
    tok = pl.BlockSpec((tm, width), lambda i: (i, 0))
    if seq is None:
        kv_spec, kv_shape = tok, out
    else:
        nblk = seq // tm
        kv_spec = pl.BlockSpec((None, width, tm), lambda i: (i // nblk, 0, i % nblk))
        kv_shape = jax.ShapeDtypeStruct((n // seq, width, seq), F32)
    return pl.pallas_call(
        functools.partial(_in_proj_kernel, kv_feature_major=seq is not None),
        grid=(n // tm,),
        in_specs=[pl.BlockSpec((tm, d), lambda i: (i, 0)),
                  _full_spec((1, d)),
                  _full_spec(w_in_bf.shape)],
        out_specs=[tok, kv_spec, kv_spec, tok, tok],
        out_shape=[out, kv_shape, kv_shape, out, out],
        compiler_params=_cparams(("parallel",)),
        name="in_proj",
    )(x2d, norm_g.reshape(1, d), w_in_bf)


def _sb_prompt_kernel(bias_ref, q_ref, k_ref, v_ref, tri_ref, o_ref, *, tq, head_dim):
    hp = pl.program_id(1)
    i = pl.program_id(2)
    nh = LANES // head_dim
    lane = lax.broadcasted_iota(jnp.int32, (1, LANES), 1)
    q = q_ref[...] * (head_dim ** -0.5 * LOG2E)
    tri = tri_ref[...]
    row = lax.broadcasted_iota(jnp.int32, (tq, tq), 0)
    col = lax.broadcasted_iota(jnp.int32, (tq, tq), 1)
    diag_valid = col < row
    in_head = [(lane >= hh * head_dim) & (lane < (hh + 1) * head_dim) for hh in range(nh)]
    bias = [bias_ref[hp * nh + hh] * LOG2E for hh in range(nh)]
    qh = [jnp.where(in_head[hh], q, 0.0).astype(BF16) for hh in range(nh)]

    def process(blocks, state):
        staged = []
        for j, masked in blocks:
            start = pl.multiple_of(j * tq, tq)
            kblk = k_ref[:, pl.ds(start, tq)].astype(BF16)
            vblk = v_ref[:, pl.ds(start, tq)].astype(BF16)
            per_head = []
            for hh in range(nh):
                z = _dot(qh[hh], kblk) + bias[hh]
                lk = _log2_keep(z)
                if masked:
                    lk = jnp.where(diag_valid, lk, 0.0)
                hi, lo = _split_bf16(lk)
                r_incl = _dot(hi, tri) + _dot(lo, tri)
                per_head.append((z, r_incl, jnp.sum(lk, axis=1, keepdims=True)))
            staged.append((vblk, masked, per_head))
        out_state = []
        for hh in range(nh):
            o_acc, carry = state[2 * hh], state[2 * hh + 1]
            for vblk, masked, per_head in staged:
                z, r_incl, tot = per_head[hh]
                w = jnp.exp2(z + r_incl + carry)
                if masked:
                    w = jnp.where(diag_valid, w, 0.0)
                o_acc = o_acc + _dot_nt(w.astype(BF16), vblk)
                carry = carry + tot
            out_state += [o_acc, carry]
        return tuple(out_state)

    state = (jnp.zeros((tq, LANES), F32), jnp.zeros((tq, 1), F32)) * nh
    state = process([(i, True)], state)
    state = lax.fori_loop(
        0, i // 2, lambda jj, st: process([(i - 1 - 2 * jj, False), (i - 2 - 2 * jj, False)], st), state)
    state = lax.fori_loop(0, i % 2, lambda jj, st: process([(0, False)], st), state)
    out = jnp.zeros((tq, LANES), F32)
    for hh in range(nh):
        out = jnp.where(in_head[hh], state[2 * hh], out)
    o_ref[...] = out


def _sb_prompt(q, k_fm, v_fm, sb_bias, batch, seq, head_dim, tq):
    n, width = q.shape
    nblk = seq // tq
    npair = width // LANES
    tri = jnp.asarray(np.tril(np.ones((tq, tq), np.float32)), BF16)
    kern = functools.partial(_sb_prompt_kernel, tq=tq, head_dim=head_dim)
    return pl.pallas_call(
        kern,
        grid=(batch, npair, nblk),
        in_specs=[pl.BlockSpec(memory_space=pltpu.SMEM),
                  pl.BlockSpec((tq, LANES), lambda b, h, i: (b * nblk + i, h)),
                  pl.BlockSpec((None, LANES, seq), lambda b, h, i: (b, h, 0)),
                  pl.BlockSpec((None, LANES, seq), lambda b, h, i: (b, h, 0)),
                  _full_spec((tq, tq))],
        out_specs=pl.BlockSpec((tq, LANES), lambda b, h, i: (b * nblk + i, h)),
        out_shape=jax.ShapeDtypeStruct((n, width), F32),
        compiler_params=_cparams(("parallel", "parallel", "arbitrary")),
        name="sb_prompt",
    )(sb_bias, q, k_fm, v_fm, tri)


def _sb_sample_kernel(pt_ref, bias_ref, qcol_ref, tri_ref, *refs, pp, heads, head_dim):
    k_refs = refs[:pp]
    v_refs = refs[pp:2 * pp]
    o_ref = refs[2 * pp]
    acc_ref, carry_ref = refs[2 * pp + 1:]
    j = pl.program_id(1)
    page = tri_ref.shape[0]
    assert heads == SUBLANES and head_dim % SUBLANES == 0
    nslab = head_dim // SUBLANES

    @pl.when(j == 0)
    def _():
        acc_ref[...] = jnp.zeros_like(acc_ref)
        carry_ref[...] = jnp.zeros_like(carry_ref)

    def slab(ref, h, r):
        lo = h * head_dim + r * SUBLANES
        return ref[lo:lo + SUBLANES, :]

    sub = lax.broadcasted_iota(jnp.int32, (SUBLANES, page), 0)
    zs = [jnp.zeros((SUBLANES, page), F32) for _ in range(pp)]
    for h in range(heads):
        qs = [slab(qcol_ref, h, r) for r in range(nslab)]
        for p in range(pp):
            s8 = slab(k_refs[p], h, 0) * qs[0]
            for r in range(1, nslab):
                s8 = s8 + slab(k_refs[p], h, r) * qs[r]
            for sh in (4, 2, 1):
                s8 = s8 + pltpu.roll(s8, sh, 0)
            zs[p] = jnp.where(sub == h, s8, zs[p])
    z = jnp.concatenate(zs, axis=0) + jnp.concatenate([bias_ref[...] * LOG2E] * pp, axis=0)
    lk = _log2_keep(z)
    hi, lo = _split_bf16(lk)
    tri = tri_ref[...]
    r_incl = _dot(hi, tri) + _dot(lo, tri)
    tot = jnp.sum(lk, axis=1, keepdims=True)
    carry = carry_ref[...]
    ws = []
    for p in range(pp):
        sl = slice(p * heads, (p + 1) * heads)
        ws.append(jnp.exp2(z[sl] + r_incl[sl] + carry))
        carry = carry + tot[sl]
    carry_ref[...] = carry
    for h in range(heads):
        acc = [slab(acc_ref, h, r) for r in range(nslab)]
        for p in range(pp):
            wb = jnp.broadcast_to(ws[p][h:h + 1, :], (SUBLANES, page))
            for r in range(nslab):
                acc[r] = acc[r] + slab(v_refs[p], h, r) * wb
        for r in range(nslab):
            lo_r = h * head_dim + r * SUBLANES
            acc_ref[lo_r:lo_r + SUBLANES, :] = acc[r]

    @pl.when(j == pl.num_programs(1) - 1)
    def _():
        o_ref[...] = jnp.sum(acc_ref[...], axis=1, keepdims=True)


def _sb_sample(q, sb_bias, cache_k, cache_v, page_table, heads, head_dim, pp):
    bd, width = q.shape
    n_phys, page = cache_k.shape[0], cache_k.shape[1]
    n_pages = page_table.shape[1]
    assert n_pages % pp == 0
    ck = jnp.transpose(cache_k, (0, 2, 3, 1)).reshape(n_phys, width, page)
    cv = jnp.transpose(cache_v, (0, 2, 3, 1)).reshape(n_phys, width, page)
    qcol = jnp.broadcast_to((q * (head_dim ** -0.5 * LOG2E))[:, :, None], (bd, width, page))
    tri = jnp.asarray(np.tril(np.ones((page, page), np.float32)), BF16)
    kern = functools.partial(_sb_sample_kernel, pp=pp, heads=heads, head_dim=head_dim)

    def page_map(p):
        return lambda b, j, pt: (pt[b, n_pages - 1 - (j * pp + p)], 0, 0)

    kv_specs = [pl.BlockSpec((None, width, page), page_map(p)) for p in range(pp)]
    grid_spec = pltpu.PrefetchScalarGridSpec(
        num_scalar_prefetch=1,
        grid=(bd, n_pages // pp),
        in_specs=[pl.BlockSpec((heads, 1), lambda b, j, pt: (0, 0)),
                  pl.BlockSpec((None, width, page), lambda b, j, pt: (b, 0, 0)),
                  pl.BlockSpec((page, page), lambda b, j, pt: (0, 0))] + kv_specs + kv_specs,
        out_specs=pl.BlockSpec((None, width, 1), lambda b, j, pt: (b, 0, 0)),
        scratch_shapes=[pltpu.VMEM((width, page), F32),
                        pltpu.VMEM((heads, 1), F32)],
    )
    out = pl.pallas_call(
        kern,
        grid_spec=grid_spec,
        out_shape=jax.ShapeDtypeStruct((bd, width, 1), F32),
        compiler_params=_cparams(("parallel", "arbitrary")),
        name="sb_sample",
    )(page_table, sb_bias.reshape(heads, 1), qcol, tri, *([ck] * pp), *([cv] * pp))
    return out.reshape(bd, width)


def _lru_gates(xc, wa_ref, ba_ref, wx_ref, bx_ref, lam_ref):
    xcb = xc.astype(BF16)
    r = jax.nn.sigmoid(_dot(xcb, wa_ref[...]) + ba_ref[...])
    ig = jax.nn.sigmoid(_dot(xcb, wx_ref[...]) + bx_ref[...])
    log_a = (-LRU_C) * r * _softplus(-lam_ref[...])
    a = jnp.exp(log_a)
    one_minus_a2 = -jnp.tanh(log_a) * (a * a + 1.0)
    u = jnp.sqrt(one_minus_a2) * (ig * xc)
    return a, u


def _lru_prompt_kernel(xl_ref, gl_ref, conv0_ref, h0_ref, cw_ref, cb_ref, wa_ref, ba_ref, wx_ref, bx_ref,
                       lam_ref, o_ref, convn_ref, hn_ref, xpad_ref, h_ref, *, ts):
    tb = pl.program_id(1)
    tail = CONV_WIDTH - 1

    @pl.when(tb == 0)
    def _():
        xpad_ref[SUBLANES - tail:SUBLANES, :] = conv0_ref[...]
        h_ref[...] = h0_ref[...]

    xl = xl_ref[...]
    xpad_ref[SUBLANES:SUBLANES + ts, :] = xl
    cw = cw_ref[...]
    xc = cb_ref[...]
    for jw in range(CONV_WIDTH):
        off = SUBLANES - tail + jw
        xc = xc + xpad_ref[off:off + ts, :] * cw[jw:jw + 1, :]
    a, u = _lru_gates(xc, wa_ref, ba_ref, wx_ref, bx_ref, lam_ref)
    rows = lax.broadcasted_iota(jnp.int32, a.shape, 0)
    d = 1
    while d < ts:
        keep = rows >= d
        a_sh = jnp.where(keep, pltpu.roll(a, d, 0), 1.0)
        u_sh = jnp.where(keep, pltpu.roll(u, d, 0), 0.0)
        u = u + a * u_sh
        a = a * a_sh
        d *= 2
    hs = a * h_ref[...] + u
    o_ref[...] = hs * _gelu_tanh(gl_ref[...])
    h_ref[...] = hs[ts - 1:ts, :]
    xpad_ref[SUBLANES - tail:SUBLANES, :] = xl[ts - tail:ts, :]

    @pl.when(tb == pl.num_programs(1) - 1)
    def _():
        convn_ref[...] = xl[ts - tail:ts, :]
        hn_ref[...] = hs[ts - 1:ts, :]


def _lru_weights(conv_w, conv_b, gate_a_w, gate_a_b, gate_x_w, gate_x_b, lru_lambda):
    nb, bd, _ = gate_a_w.shape
    w = nb * bd
    eye = jnp.eye(nb, dtype=F32)
    wa = jnp.einsum('nde,nm->ndme', gate_a_w, eye).reshape(w, w).astype(BF16)
    wx = jnp.einsum('nde,nm->ndme', gate_x_w, eye).reshape(w, w).astype(BF16)
    return (conv_w, conv_b.reshape(1, w), wa, gate_a_b.reshape(1, w), wx, gate_x_b.reshape(1, w),
            lru_lambda.reshape(1, w))


def _lru_prompt(xl, gl, conv0, h0, lw, batch, seq, ts):
    n, w = xl.shape
    nblk = seq // ts
    tail = CONV_WIDTH - 1
    kern = functools.partial(_lru_prompt_kernel, ts=ts)
    tok = pl.BlockSpec((ts, w), lambda b, t: (b * nblk + t, 0))
    return pl.pallas_call(
        kern,
        grid=(batch, nblk),
        in_specs=[tok, tok,
                  pl.BlockSpec((None, tail, w), lambda b, t: (b, 0, 0)),
                  pl.BlockSpec((None, 1, w), lambda b, t: (b, 0, 0)),
                  _full_spec((CONV_WIDTH, w)), _full_spec((1, w)),
                  _full_spec((w, w)), _full_spec((1, w)),
                  _full_spec((w, w)), _full_spec((1, w)), _full_spec((1, w))],
        out_specs=[tok,
                   pl.BlockSpec((None, tail, w), lambda b, t: (b, 0, 0)),
                   pl.BlockSpec((None, 1, w), lambda b, t: (b, 0, 0))],
        out_shape=[jax.ShapeDtypeStruct((n, w), F32),
                   jax.ShapeDtypeStruct((batch, tail, w), F32),
                   jax.ShapeDtypeStruct((batch, 1, w), F32)],
        scratch_shapes=[pltpu.VMEM((SUBLANES + ts, w), F32), pltpu.VMEM((1, w), F32)],
        compiler_params=_cparams(("parallel", "arbitrary")),
        name="lru_prompt",
    )(xl, gl, conv0, h0.reshape(batch, 1, w), *lw)


def _lru_sample_kernel(xl_ref, gl_ref, c0_ref, c1_ref, c2_ref, h0_ref, cw_ref, cb_ref, wa_ref, ba_ref,
                       wx_ref, bx_ref, lam_ref, o_ref, hn_ref):
    cw = cw_ref[...]
    xl = xl_ref[...]
    xc = cb_ref[...] + c0_ref[...] * cw[0:1, :]
    xc = xc + c1_ref[...] * cw[1:2, :]
    xc = xc + c2_ref[...] * cw[2:3, :]
    xc = xc + xl * cw[3:4, :]
    a, u = _lru_gates(xc, wa_ref, ba_ref, wx_ref, bx_ref, lam_ref)
    h = a * h0_ref[...] + u
    hn_ref[...] = h
    o_ref[...] = h * _gelu_tanh(gl_ref[...])


def _lru_sample(xl, gl, state_conv, h0, lw):
    bd, w = xl.shape
    full = _full_spec((bd, w))
    return pl.pallas_call(
        _lru_sample_kernel,
        grid=(1,),
        in_specs=[full] * 6 + [_full_spec((CONV_WIDTH, w)), _full_spec((1, w)),
                               _full_spec((w, w)), _full_spec((1, w)),
                               _full_spec((w, w)), _full_spec((1, w)), _full_spec((1, w))],
        out_specs=[full, full],
        out_shape=[jax.ShapeDtypeStruct((bd, w), F32)] * 2,
        compiler_params=_cparams(("arbitrary",)),
        name="lru_sample",
    )(xl, gl, state_conv[:, 0], state_conv[:, 1], state_conv[:, 2], h0, *lw)


def _mix_peer_q_kernel(oa_ref, ol_ref, x_ref, ga_ref, gl_ref, wa_ref, wl_ref, gf_ref, wq_ref, k1_ref, k2_ref,
                       x1_ref, xn_ref, s1_ref, s2_ref, *, heads, half):
    na = _rms(oa_ref[...], ga_ref[...]).astype(BF16)
    nl = _rms(ol_ref[...], gl_ref[...]).astype(BF16)
    x1 = x_ref[...] + (_dot(na, wa_ref[...]) + _dot(nl, wl_ref[...]))
    x1_ref[...] = x1
    xn = _rms(x1, gf_ref[...]).astype(BF16)
    xn_ref[...] = xn
    qp = _dot(xn, wq_ref[...]).astype(BF16)
    k1 = k1_ref[...]
    k2 = k2_ref[...]
    for h in range(heads):
        base = h * 2 * half
        s1_ref[h] = _dot_nt(k1, qp[:, base:base + half])
        s2_ref[h] = _dot_nt(k2, qp[:, base + half:base + 2 * half])


def _mix_peer_q(o_att, o_lru, x2d, g_att, g_lru, w_out_bf, g_ffn, wq_bf, k1_bf, k2_bf, heads, tm):
    n, d = x2d.shape
    wa_w = o_att.shape[1]
    nk, half = k1_bf.shape
    kern = functools.partial(_mix_peer_q_kernel, heads=heads, half=half)
    tok = lambda w: pl.BlockSpec((tm, w), lambda i: (i, 0))
    sspec = pl.BlockSpec((heads, nk, tm), lambda i: (0, 0, i))
    return pl.pallas_call(
        kern,
        grid=(n // tm,),
        in_specs=[tok(wa_w), tok(o_lru.shape[1]), tok(d),
                  _full_spec((1, wa_w)), _full_spec((1, o_lru.shape[1])),
                  _full_spec((wa_w, d)), _full_spec((o_lru.shape[1], d)),
                  _full_spec((1, d)), _full_spec(wq_bf.shape),
                  _full_spec(k1_bf.shape), _full_spec(k2_bf.shape)],
        out_specs=[tok(d), tok(d), sspec, sspec],
        out_shape=[jax.ShapeDtypeStruct((n, d), F32), jax.ShapeDtypeStruct((n, d), BF16),
                   jax.ShapeDtypeStruct((heads, nk, n), F32), jax.ShapeDtypeStruct((heads, nk, n), F32)],
        compiler_params=_cparams(("parallel",)),
        name="mix_peer_q",
    )(o_att, o_lru, x2d, g_att.reshape(1, -1), g_lru.reshape(1, -1),
      w_out_bf[:wa_w], w_out_bf[wa_w:], g_ffn.reshape(1, d), wq_bf, k1_bf, k2_bf)


_CAND = [(i, j) for i in range(PEER_TOPK) for j in range(PEER_TOPK) if (i + 1) * (j + 1) <= PEER_TOPK]
_CAND_ROWS = -(-len(_CAND) // SUBLANES) * SUBLANES


def _oe_merge(lo, hi, r):
    step = r * 2
    if step < hi - lo:
        yield from _oe_merge(lo, hi, step)
        yield from _oe_merge(lo + r, hi, step)
        yield from [(i, i + r) for i in range(lo + r, hi - r, step)]
    else:
        yield (lo, lo + r)


def _oe_sort(lo, hi):
    if hi - lo >= 1:
        mid = lo + (hi - lo) // 2
        yield from _oe_sort(lo, mid)
        yield from _oe_sort(mid + 1, hi)
        yield from _oe_merge(lo, hi, 1)


def _cmp_exchange(vals, i, j):
    a, b = vals[i], vals[j]
    if b is None:
        return
    if a is None:
        vals[i], vals[j] = b, None
        return
    vals[i], vals[j] = jnp.maximum(a, b), jnp.minimum(a, b)


def _top_sorted(s, k):
    assert k == PEER_TOPK and s.shape[0] % SUBLANES == 0
    nslab = s.shape[0] // SUBLANES
    n = 1 << (nslab - 1).bit_length()
    assert n <= k
    vals = [s[i * SUBLANES:(i + 1) * SUBLANES, :] for i in range(nslab)] + [None] * (k - nslab)
    for i, j in _oe_sort(0, n - 1):
        _cmp_exchange(vals, i, j)
    for shift in (4, 2, 1):
        other = [None if v is None else pltpu.roll(v, shift, 0) for v in vals]
        merged = []
        for i in range(k):
            a, b = vals[i], other[k - 1 - i]
            merged.append(b if a is None else a if b is None else jnp.maximum(a, b))
        vals = merged
        dist = k // 2
        while dist >= 1:
            for i in range(k):
                if i & dist == 0:
                    _cmp_exchange(vals, i, i + dist)
            dist //= 2
    return vals


def _peer_select_kernel(s1_ref, s2_ref, ci_ref, cj_ref, a1_ref, a2_ref, tau_ref):
    s1 = s1_ref[...]
    s2 = s2_ref[...]
    t = s1.shape[1]
    v1 = [v[0:1, :] for v in _top_sorted(s1, PEER_TOPK)]
    v2 = [v[0:1, :] for v in _top_sorted(s2, PEER_TOPK)]
    reps = t // LANES
    ci = jnp.concatenate([ci_ref[...]] * reps, axis=1) if reps > 1 else ci_ref[...]
    cj = jnp.concatenate([cj_ref[...]] * reps, axis=1) if reps > 1 else cj_ref[...]
    b1 = jnp.zeros((_CAND_ROWS, t), F32)
    b2 = jnp.zeros((_CAND_ROWS, t), F32)
    for r in range(PEER_TOPK):
        b1 = jnp.where(ci == r, jnp.exp2((v1[r] - v1[0]) * LOG2E), b1)
        b2 = jnp.where(cj == r, jnp.exp2((v2[r] - v2[0]) * LOG2E), b2)
    top = _top_sorted(b1 * b2, PEER_TOPK)
    zsum = top[0]
    for r in range(1, PEER_TOPK):
        zsum = zsum + top[r]
    scale = 0.5 / zsum[0:1, :]
    tau_ref[...] = _top_sorted((b1 * scale) * b2, PEER_TOPK)[PEER_TOPK - 1][0:1, :]
    a1_ref[...] = jnp.exp2((s1 - v1[0]) * LOG2E) * scale
    a2_ref[...] = jnp.exp2((s2 - v2[0]) * LOG2E)


def _peer_select(s1, s2, ts):
    heads, nk, n = s1.shape
    ci = np.full((_CAND_ROWS, LANES), -1, np.int32)
    cj = np.full((_CAND_ROWS, LANES), -1, np.int32)
    for r, (i, j) in enumerate(_CAND):
        ci[r, :] = i
        cj[r, :] = j
    tab = pl.BlockSpec((None, nk, ts), lambda t, h: (h, 0, t))
    return pl.pallas_call(
        _peer_select_kernel,
        grid=(n // ts, heads),
        in_specs=[tab, tab, _full_spec(ci.shape), _full_spec(cj.shape)],
        out_specs=[tab, tab, pl.BlockSpec((None, 1, ts), lambda t, h: (h, 0, t))],
        out_shape=[jax.ShapeDtypeStruct((heads, nk, n), F32), jax.ShapeDtypeStruct((heads, nk, n), F32),
                   jax.ShapeDtypeStruct((heads, 1, n), F32)],
        compiler_params=_cparams(("parallel", "parallel")),
        name="peer_select",
    )(s1, s2, jnp.asarray(ci), jnp.asarray(cj))


def _peer_dense_kernel(xn_ref, x1_ref, a1_ref, a2_ref, tau_ref, u_ref, vt_ref, x2_ref,
                       acc_ref, act_ref, h_ref, *, heads, nk, ec):
    c = pl.program_id(1)
    tb = xn_ref.shape[0]
    n_e1 = ec // nk
    c_in = math.sqrt(2.0 / math.pi)

    @pl.when(c == 0)
    def _():
        acc_ref[...] = jnp.zeros_like(acc_ref)

    parts = 4
    e_rows = ec // parts
    r_per = n_e1 // parts

    def act_part(q):
        return _dot_nt(u_ref[q * e_rows:(q + 1) * e_rows, :], xn_ref[...])

    def gate_part(q, act):
        for r in range(q * r_per, (q + 1) * r_per):
            for lt in range(tb // LANES):
                ls = slice(lt * LANES, (lt + 1) * LANES)
                tau = [jnp.broadcast_to(tau_ref[h, :, ls], (SUBLANES, LANES)) for h in range(heads)]
                row = [jnp.broadcast_to(a1_ref[h, r:r + 1, ls], (SUBLANES, LANES)) for h in range(heads)]
                for sb in range(0, nk, 2 * SUBLANES):
                    halves = []
                    for s0 in (sb, sb + SUBLANES):
                        gate = None
                        for h in range(heads):
                            e = a2_ref[h, s0:s0 + SUBLANES, ls] * row[h]
                            kept = jnp.where(e >= tau[h], e, 0.0)
                            gate = kept if gate is None else gate + kept
                        lo = (r - q * r_per) * nk + s0
                        x = act[lo:lo + SUBLANES, ls]
                        inner = x * ((x * x) * (0.044715 * c_in) + c_in)
                        halves.append(gate * (x * (1.0 + jnp.tanh(inner))))
                    h_ref[r * nk + sb:r * nk + sb + 2 * SUBLANES, ls] = (
                        jnp.concatenate(halves, axis=0).astype(BF16))

    acts = [act_part(0), act_part(1)]
    for q in range(parts):
        gate_part(q, acts[q])
        if q + 2 < parts:
            acts.append(act_part(q + 2))
    acc_ref[...] += _dot(vt_ref[...], h_ref[...])

    @pl.when(c == pl.num_programs(1) - 1)
    def _():
        x2_ref[...] = x1_ref[...] + acc_ref[...].T


def _peer_dense(xn_bf, x1, a1, a2, tau, u_bf, vt_bf, tb, ec):
    n, d = x1.shape
    heads, nk, _ = a1.shape
    n_exp = u_bf.shape[0]
    n_e1 = ec // nk
    assert n_e1 % SUBLANES == 0 and n_exp % ec == 0 and n % tb == 0
    kern = functools.partial(_peer_dense_kernel, heads=heads, nk=nk, ec=ec)
    return pl.pallas_call(
        kern,
        grid=(n // tb, n_exp // ec),
        in_specs=[pl.BlockSpec((tb, d), lambda i, c: (i, 0)),
                  pl.BlockSpec((tb, d), lambda i, c: (i, 0)),
                  pl.BlockSpec((heads, n_e1, tb), lambda i, c: (0, c, i)),
                  pl.BlockSpec((heads, nk, tb), lambda i, c: (0, 0, i)),
                  pl.BlockSpec((heads, 1, tb), lambda i, c: (0, 0, i)),
                  pl.BlockSpec((ec, d), lambda i, c: (c, 0)),
                  pl.BlockSpec((d, ec), lambda i, c: (0, c))],
        out_specs=pl.BlockSpec((tb, d), lambda i, c: (i, 0)),
        out_shape=jax.ShapeDtypeStruct((n, d), F32),
        scratch_shapes=[pltpu.VMEM((d, tb), F32), pltpu.VMEM((ec, tb), F32), pltpu.VMEM((ec, tb), BF16)],
        compiler_params=_cparams(("parallel", "arbitrary")),
        name="peer_dense",
    )(xn_bf, x1, a1, a2, tau, u_bf, vt_bf)


def _ple_final_kernel(x_ref, p_ref, gp_ref, wg_ref, wp_ref, gf_ref, y_ref):
    x = x_ref[...]
    gate = jax.nn.sigmoid(_dot(_rms(x, gp_ref[...]).astype(BF16), wg_ref[...]))
    x3 = x + gate * _dot(p_ref[...].astype(BF16), wp_ref[...])
    y_ref[...] = _rms(x3, gf_ref[...])


def _ple_final(x2, p2d, g_ple, wg_bf, wp_bf, g_final, tm):
    n, d = x2.shape
    pd = p2d.shape[1]
    return pl.pallas_call(
        _ple_final_kernel,
        grid=(n // tm,),
        in_specs=[pl.BlockSpec((tm, d), lambda i: (i, 0)),
                  pl.BlockSpec((tm, pd), lambda i: (i, 0)),
                  _full_spec((1, d)), _full_spec((d, d)), _full_spec((pd, d)), _full_spec((1, d))],
        out_specs=pl.BlockSpec((tm, d), lambda i: (i, 0)),
        out_shape=jax.ShapeDtypeStruct((n, d), F32),
        compiler_params=_cparams(("parallel",)),
        name="ple_final",
    )(x2, p2d, g_ple.reshape(1, d), wg_bf, wp_bf, g_final.reshape(1, d))


def _pick(n, pref):
    t = min(n, pref)
    while n % t:
        t //= 2
    return t


def _tiles(n_tokens, seq):
    return dict(
        tm=_pick(n_tokens, 512),
        tq=_pick(seq, 256),
        ts_lru=_pick(seq, 256),
        ts_sel=_pick(n_tokens, 512),
        tb=_pick(n_tokens, 512),
        ec=1024,
    )


def _tail(x1, xn_bf, s1, s2, p2d, wts, tiles):
    a1, a2, tau = _peer_select(s1, s2, tiles["ts_sel"])
    x2 = _peer_dense(xn_bf, x1, a1, a2, tau, wts["u"], wts["vt"], tiles["tb"], tiles["ec"])
    return _ple_final(x2, p2d, wts["g_ple"], wts["wg"], wts["wp"], wts["g_final"], tiles["tm"])


def kernel(x_prompt, x_sample, cache_k, cache_v, state_conv, state_h, page_table, p_prompt, p_sample, norm_mix, w_in, sb_bias, conv_w, conv_b, gate_a_w, gate_a_b, gate_x_w, gate_x_b, lru_lambda, norm_attn_out, norm_lru_out, w_out, norm_ffn, peer_w_q, peer_keys1, peer_keys2, peer_u, peer_v, norm_ple, ple_w_gate, ple_w_proj, norm_final):
    depth = w_in.shape[0]
    assert depth == 1, "single-layer trunk"
    bp, seq, d = x_prompt.shape
    bd, dseq, _ = x_sample.shape
    assert dseq == 1, "one new token per sampled sequence"
    heads = sb_bias.shape[1]
    head_dim = cache_k.shape[-1]
    sbw = heads * head_dim
    lruw = lru_lambda.shape[1]
    peer_heads = peer_w_q.shape[2] // (2 * peer_keys1.shape[2])

    li = 0
    w_in_bf = w_in[li].astype(BF16)
    lw = _lru_weights(conv_w[li], conv_b[li], gate_a_w[li], gate_a_b[li], gate_x_w[li], gate_x_b[li],
                      lru_lambda[li])
    w_out_bf = w_out[li].astype(BF16)
    wq_bf = peer_w_q[li].astype(BF16)
    k1_bf = peer_keys1[li].astype(BF16)
    k2_bf = peer_keys2[li].astype(BF16)
    wts = dict(u=peer_u[li].astype(BF16), vt=peer_v[li].T.astype(BF16), g_ple=norm_ple[li],
               wg=ple_w_gate[li].astype(BF16), wp=ple_w_proj[li].astype(BF16), g_final=norm_final)

    n_p = bp * seq
    tp = _tiles(n_p, seq)
    xp2d = x_prompt.reshape(n_p, d)
    q, k_fm, v_fm, xl, gl = _in_proj(xp2d, norm_mix[li], w_in_bf, tp["tm"], seq=seq)
    o_att = _sb_prompt(q, k_fm, v_fm, sb_bias[li], bp, seq, head_dim, tp["tq"])
    conv0 = jnp.zeros((bp, CONV_WIDTH - 1, lruw), F32)
    h0 = jnp.zeros((bp, lruw), F32)
    o_lru, conv_p, h_p = _lru_prompt(xl, gl, conv0, h0, lw, bp, seq, tp["ts_lru"])
    x1, xn_bf, s1, s2 = _mix_peer_q(o_att, o_lru, xp2d, norm_attn_out[li], norm_lru_out[li], w_out_bf,
                                    norm_ffn[li], wq_bf, k1_bf, k2_bf, peer_heads, tp["tm"])
    y_p = _tail(x1, xn_bf, s1, s2, p_prompt[li].reshape(n_p, -1), wts, tp)

    ts_ = _tiles(bd, 1)
    xs2d = x_sample.reshape(bd, d)
    qs, ks, vs, xls, gls = _in_proj(xs2d, norm_mix[li], w_in_bf, ts_["tm"])
    o_att_s = _sb_sample(qs, sb_bias[li], cache_k[li], cache_v[li], page_table, heads, head_dim,
                         pp=math.gcd(page_table.shape[1], 16))
    o_lru_s, h_s = _lru_sample(xls, gls, state_conv[li], state_h[li], lw)
    conv_s = jnp.concatenate([state_conv[li][:, 1:], xls[:, None, :]], axis=1)
    x1s, xns_bf, s1s, s2s = _mix_peer_q(o_att_s, o_lru_s, xs2d, norm_attn_out[li], norm_lru_out[li], w_out_bf,
                                        norm_ffn[li], wq_bf, k1_bf, k2_bf, peer_heads, ts_["tm"])
    y_s = _tail(x1s, xns_bf, s1s, s2s, p_sample[li].reshape(bd, -1), wts, ts_)

    return (y_p.reshape(bp, seq, d), y_s.reshape(bd, 1, d),
            jnp.transpose(k_fm.reshape(bp, heads, head_dim, seq), (0, 3, 1, 2))[None],
            jnp.transpose(v_fm.reshape(bp, heads, head_dim, seq), (0, 3, 1, 2))[None],
            conv_p[None], h_p.reshape(1, bp, lruw),
            ks.reshape(1, bd, 1, heads, head_dim), vs.reshape(1, bd, 1, heads, head_dim),
            conv_s[None], h_s[None])
```

```python
import functools
import math

import numpy as np
import jax
import jax.numpy as jnp
from jax import lax
from jax.experimental import pallas as pl
from jax.experimental.pallas import tpu as pltpu

F32 = jnp.float32
BF16 = jnp.bfloat16

RMS_EPS = 1e-6
LRU_C = 8.0
CONV_WIDTH = 4
PEER_TOPK = 16
LOG2E = 1.4426950408889634
NEG_INF = float("-inf")

LANES = 128
SUBLANES = 8
VMEM_LIMIT = 56 * 1024 * 1024


def _cparams(sem, flags=None):
    return pltpu.CompilerParams(dimension_semantics=sem, vmem_limit_bytes=VMEM_LIMIT, flags=flags)


def _rms(x, g):
    ms = jnp.mean(x * x, axis=-1, keepdims=True)
    return x * lax.rsqrt(ms + RMS_EPS) * g


def _softplus(z):
    return jnp.maximum(z, 0.0) + jnp.log1p(jnp.exp(-jnp.abs(z)))


def _log2_keep(z2):
    nz = -z2
    return jnp.minimum(nz, 0.0) - jnp.log2(1.0 + jnp.exp2(jnp.minimum(z2, nz)))


def _gelu_tanh(x):
    c = math.sqrt(2.0 / math.pi)
    return 0.5 * x * (1.0 + jnp.tanh(c * (x + 0.044715 * (x * x * x))))


def _dot(a, b):
    return jnp.dot(a, b, preferred_element_type=F32)


def _dot_nt(a, b):
    return lax.dot_general(a, b, (((1,), (1,)), ((), ())), preferred_element_type=F32)


def _split_bf16(x):
    hi = x.astype(BF16)
    lo = (x - hi.astype(F32)).astype(BF16)
    return hi, lo


def _full_spec(shape):
    nd = len(shape)
    return pl.BlockSpec(shape, lambda *_: (0,) * nd)


def _in_proj_kernel(x_ref, g_ref, w_ref, q_ref, k_ref, v_ref, xl_ref, gl_ref, *, kv_feature_major):
    xn = _rms(x_ref[...], g_ref[...]).astype(BF16)
    proj = _dot(xn, w_ref[...])
    width = q_ref.shape[-1]
    for i, ref in enumerate((q_ref, k_ref, v_ref, xl_ref, gl_ref)):
        part = proj[:, i * width:(i + 1) * width]
        ref[...] = part.T if (kv_feature_major and i in (1, 2)) else part


def _in_proj(x2d, norm_g, w_in_bf, tm, seq=None):
    n, d = x2d.shape
    width = w_in_bf.shape[1] // 5
    out = jax.ShapeDtypeStruct((n, width), F32)
    tok = pl.BlockSpec((tm, width), lambda i: (i, 0))
    if seq is None:
        kv_spec, kv_shape = tok, out
    else:
        nblk = seq // tm
        kv_spec = pl.BlockSpec((None, width, tm), lambda i: (i // nblk, 0, i % nblk))
        kv_shape = jax.ShapeDtypeStruct((n // seq, width, seq), F32)
    return pl.pallas_call(
        functools.partial(_in_proj_kernel, kv_feature_major=seq is not None),
        grid=(n // tm,),
        in_specs=[pl.BlockSpec((tm, d), lambda i: (i, 0)),
                  _full_spec((1, d)),
                  _full_spec(w_in_bf.shape)],
        out_specs=[tok, kv_spec, kv_spec, tok, tok],
        out_shape=[out, kv_shape, kv_shape, out, out],
        compiler_params=_cparams(("parallel",)),
        name="in_proj",
    )(x2d, norm_g.reshape(1, d), w_in_bf)


def _sb_prompt_kernel(bias_ref, q_ref, k_ref, v_ref, tri_ref, o_ref, *, tq, head_dim):
    hp = pl.program_id(1)
    i = pl.program_id(2)
    nh = LANES // head_dim
    lane = lax.broadcasted_iota(jnp.int32, (1, LANES), 1)
    q = q_ref[...] * (head_dim ** -0.5 * LOG2E)
    tri = tri_ref[...]
    row = lax.broadcasted_iota(jnp.int32, (tq, tq), 0)
    col = lax.broadcasted_iota(jnp.int32, (tq, tq), 1)
    diag_valid = col < row
    in_head = [(lane >= hh * head_dim) & (lane < (hh + 1) * head_dim) for hh in range(nh)]
    bias = [bias_ref[hp * nh + hh] * LOG2E for hh in range(nh)]
    qh = [jnp.where(in_head[hh], q, 0.0).astype(BF16) for hh in range(nh)]

    def process(blocks, state):
        staged = []
        for j, masked in blocks:
            start = pl.multiple_of(j * tq, tq)
            kblk = k_ref[:, pl.ds(start, tq)].astype(BF16)
            vblk = v_ref[:, pl.ds(start, tq)].astype(BF16)
            per_head = []
            for hh in range(nh):
                z = _dot(qh[hh], kblk) + bias[hh]
                lk = _log2_keep(z)
                if masked:
                    lk = jnp.where(diag_valid, lk, 0.0)
                hi, lo = _split_bf16(lk)
                r_incl = _dot(hi, tri) + _dot(lo, tri)
                per_head.append((z, r_incl, jnp.sum(lk, axis=1, keepdims=True)))
            staged.append((vblk, masked, per_head))
        out_state = []
        for hh in range(nh):
            o_acc, carry = state[2 * hh], state[2 * hh + 1]
            for vblk, masked, per_head in staged:
                z, r_incl, tot = per_head[hh]
                w = jnp.exp2(z + r_incl + carry)
                if masked:
                    w = jnp.where(diag_valid, w, 0.0)
                o_acc = o_acc + _dot_nt(w.astype(BF16), vblk)
                carry = carry + tot
            out_state += [o_acc, carry]
        return tuple(out_state)

    state = (jnp.zeros((tq, LANES), F32), jnp.zeros((tq, 1), F32)) * nh
    state = process([(i, True)], state)
    state = lax.fori_loop(
        0, i // 2, lambda jj, st: process([(i - 1 - 2 * jj, False), (i - 2 - 2 * jj, False)], st), state)
    state = lax.fori_loop(0, i % 2, lambda jj, st: process([(0, False)], st), state)
    out = jnp.zeros((tq, LANES), F32)
    for hh in range(nh):
        out = jnp.where(in_head[hh], state[2 * hh], out)
    o_ref[...] = out


def _sb_prompt(q, k_fm, v_fm, sb_bias, batch, seq, head_dim, tq):
    n, width = q.shape
    nblk = seq // tq
    npair = width // LANES
    tri = jnp.asarray(np.tril(np.ones((tq, tq), np.float32)), BF16)
    kern = functools.partial(_sb_prompt_kernel, tq=tq, head_dim=head_dim)
    return pl.pallas_call(
        kern,
        grid=(batch, npair, nblk),
        in_specs=[pl.BlockSpec(memory_space=pltpu.SMEM),
                  pl.BlockSpec((tq, LANES), lambda b, h, i: (b * nblk + i, h)),
                  pl.BlockSpec((None, LANES, seq), lambda b, h, i: (b, h, 0)),
                  pl.BlockSpec((None, LANES, seq), lambda b, h, i: (b, h, 0)),
                  _full_spec((tq, tq))],
        out_specs=pl.BlockSpec((tq, LANES), lambda b, h, i: (b * nblk + i, h)),
        out_shape=jax.ShapeDtypeStruct((n, width), F32),
        compiler_params=_cparams(("parallel", "parallel", "arbitrary")),
        name="sb_prompt",
    )(sb_bias, q, k_fm, v_fm, tri)


def _sb_sample_kernel(pt_ref, bias_ref, qcol_ref, tri_ref, *refs, pp, heads, head_dim):
    k_refs = refs[:pp]
    v_refs = refs[pp:2 * pp]
    o_ref = refs[2 * pp]
    acc_ref, carry_ref = refs[2 * pp + 1:]
    j = pl.program_id(1)
    page = tri_ref.shape[0]
    assert heads == SUBLANES and head_dim % SUBLANES == 0
    nslab = head_dim // SUBLANES

    @pl.when(j == 0)
    def _():
        acc_ref[...] = jnp.zeros_like(acc_ref)
        carry_ref[...] = jnp.zeros_like(carry_ref)

    def slab(ref, h, r):
        lo = h * head_dim + r * SUBLANES
        return ref[lo:lo + SUBLANES, :]

    sub = lax.broadcasted_iota(jnp.int32, (SUBLANES, page), 0)
    zs = [jnp.zeros((SUBLANES, page), F32) for _ in range(pp)]
    for h in range(heads):
        qs = [slab(qcol_ref, h, r) for r in range(nslab)]
        for p in range(pp):
            s8 = slab(k_refs[p], h, 0) * qs[0]
            for r in range(1, nslab):
                s8 = s8 + slab(k_refs[p], h, r) * qs[r]
            for sh in (4, 2, 1):
                s8 = s8 + pltpu.roll(s8, sh, 0)
            zs[p] = jnp.where(sub == h, s8, zs[p])
    z = jnp.concatenate(zs, axis=0) + jnp.concatenate([bias_ref[...] * LOG2E] * pp, axis=0)
    lk = _log2_keep(z)
    hi, lo = _split_bf16(lk)
    tri = tri_ref[...]
    r_incl = _dot(hi, tri) + _dot(lo, tri)
    tot = jnp.sum(lk, axis=1, keepdims=True)
    carry = carry_ref[...]
    ws = []
    for p in range(pp):
        sl = slice(p * heads, (p + 1) * heads)
        ws.append(jnp.exp2(z[sl] + r_incl[sl] + carry))
        carry = carry + tot[sl]
    carry_ref[...] = carry
    for h in range(heads):
        acc = [slab(acc_ref, h, r) for r in range(nslab)]
        for p in range(pp):
            wb = jnp.broadcast_to(ws[p][h:h + 1, :], (SUBLANES, page))
            for r in range(nslab):
                acc[r] = acc[r] + slab(v_refs[p], h, r) * wb
        for r in range(nslab):
            lo_r = h * head_dim + r * SUBLANES
            acc_ref[lo_r:lo_r + SUBLANES, :] = acc[r]

    @pl.when(j == pl.num_programs(1) - 1)
    def _():
        o_ref[...] = jnp.sum(acc_ref[...], axis=1, keepdims=True)


def _sb_sample(q, sb_bias, cache_k, cache_v, page_table, heads, head_dim, pp):
    bd, width = q.shape
    n_phys, page = cache_k.shape[0], cache_k.shape[1]
    n_pages = page_table.shape[1]
    assert n_pages % pp == 0
    ck = jnp.transpose(cache_k, (0, 2, 3, 1)).reshape(n_phys, width, page)
    cv = jnp.transpose(cache_v, (0, 2, 3, 1)).reshape(n_phys, width, page)
    qcol = jnp.broadcast_to((q * (head_dim ** -0.5 * LOG2E))[:, :, None], (bd, width, page))
    tri = jnp.asarray(np.tril(np.ones((page, page), np.float32)), BF16)
    kern = functools.partial(_sb_sample_kernel, pp=pp, heads=heads, head_dim=head_dim)

    def page_map(p):
        return lambda b, j, pt: (pt[b, n_pages - 1 - (j * pp + p)], 0, 0)

    kv_specs = [pl.BlockSpec((None, width, page), page_map(p)) for p in range(pp)]
    grid_spec = pltpu.PrefetchScalarGridSpec(
        num_scalar_prefetch=1,
        grid=(bd, n_pages // pp),
        in_specs=[pl.BlockSpec((heads, 1), lambda b, j, pt: (0, 0)),
                  pl.BlockSpec((None, width, page), lambda b, j, pt: (b, 0, 0)),
                  pl.BlockSpec((page, page), lambda b, j, pt: (0, 0))] + kv_specs + kv_specs,
        out_specs=pl.BlockSpec((None, width, 1), lambda b, j, pt: (b, 0, 0)),
        scratch_shapes=[pltpu.VMEM((width, page), F32),
                        pltpu.VMEM((heads, 1), F32)],
    )
    out = pl.pallas_call(
        kern,
        grid_spec=grid_spec,
        out_shape=jax.ShapeDtypeStruct((bd, width, 1), F32),
        compiler_params=_cparams(("parallel", "arbitrary")),
        name="sb_sample",
    )(page_table, sb_bias.reshape(heads, 1), qcol, tri, *([ck] * pp), *([cv] * pp))
    return out.reshape(bd, width)


def _lru_gates(xc, wa_ref, ba_ref, wx_ref, bx_ref, lam_ref):
    xcb = xc.astype(BF16)
    r = jax.nn.sigmoid(_dot(xcb, wa_ref[...]) + ba_ref[...])
    ig = jax.nn.sigmoid(_dot(xcb, wx_ref[...]) + bx_ref[...])
    log_a = (-LRU_C) * r * _softplus(-lam_ref[...])
    a = jnp.exp(log_a)
    one_minus_a2 = -jnp.tanh(log_a) * (a * a + 1.0)
    u = jnp.sqrt(one_minus_a2) * (ig * xc)
    return a, u


def _lru_prompt_kernel(xl_ref, gl_ref, conv0_ref, h0_ref, cw_ref, cb_ref, wa_ref, ba_ref, wx_ref, bx_ref,
                       lam_ref, o_ref, convn_ref, hn_ref, xpad_ref, h_ref, *, ts):
    tb = pl.program_id(1)
    tail = CONV_WIDTH - 1

    @pl.when(tb == 0)
    def _():
        xpad_ref[SUBLANES - tail:SUBLANES, :] = conv0_ref[...]
        h_ref[...] = h0_ref[...]

    xl = xl_ref[...]
    xpad_ref[SUBLANES:SUBLANES + ts, :] = xl
    cw = cw_ref[...]
    xc = cb_ref[...]
    for jw in range(CONV_WIDTH):
        off = SUBLANES - tail + jw
        xc = xc + xpad_ref[off:off + ts, :] * cw[jw:jw + 1, :]
    a, u = _lru_gates(xc, wa_ref, ba_ref, wx_ref, bx_ref, lam_ref)
    rows = lax.broadcasted_iota(jnp.int32, a.shape, 0)
    d = 1
    while d < ts:
        keep = rows >= d
        a_sh = jnp.where(keep, pltpu.roll(a, d, 0), 1.0)
        u_sh = jnp.where(keep, pltpu.roll(u, d, 0), 0.0)
        u = u + a * u_sh
        a = a * a_sh
        d *= 2
    hs = a * h_ref[...] + u
    o_ref[...] = hs * _gelu_tanh(gl_ref[...])
    h_ref[...] = hs[ts - 1:ts, :]
    xpad_ref[SUBLANES - tail:SUBLANES, :] = xl[ts - tail:ts, :]

    @pl.when(tb == pl.num_programs(1) - 1)
    def _():
        convn_ref[...] = xl[ts - tail:ts, :]
        hn_ref[...] = hs[ts - 1:ts, :]


def _lru_weights(conv_w, conv_b, gate_a_w, gate_a_b, gate_x_w, gate_x_b, lru_lambda):
    nb, bd, _ = gate_a_w.shape
    w = nb * bd
    eye = jnp.eye(nb, dtype=F32)
    wa = jnp.einsum('nde,nm->ndme', gate_a_w, eye).reshape(w, w).astype(BF16)
    wx = jnp.einsum('nde,nm->ndme', gate_x_w, eye).reshape(w, w).astype(BF16)
    return (conv_w, conv_b.reshape(1, w), wa, gate_a_b.reshape(1, w), wx, gate_x_b.reshape(1, w),
            lru_lambda.reshape(1, w))


def _lru_prompt(xl, gl, conv0, h0, lw, batch, seq, ts):
    n, w = xl.shape
    nblk = seq // ts
    tail = CONV_WIDTH - 1
    kern = functools.partial(_lru_prompt_kernel, ts=ts)
    tok = pl.BlockSpec((ts, w), lambda b, t: (b * nblk + t, 0))
    return pl.pallas_call(
        kern,
        grid=(batch, nblk),
        in_specs=[tok, tok,
                  pl.BlockSpec((None, tail, w), lambda b, t: (b, 0, 0)),
                  pl.BlockSpec((None, 1, w), lambda b, t: (b, 0, 0)),
                  _full_spec((CONV_WIDTH, w)), _full_spec((1, w)),
                  _full_spec((w, w)), _full_spec((1, w)),
                  _full_spec((w, w)), _full_spec((1, w)), _full_spec((1, w))],
        out_specs=[tok,
                   pl.BlockSpec((None, tail, w), lambda b, t: (b, 0, 0)),
                   pl.BlockSpec((None, 1, w), lambda b, t: (b, 0, 0))],
        out_shape=[jax.ShapeDtypeStruct((n, w), F32),
                   jax.ShapeDtypeStruct((batch, tail, w), F32),
                   jax.ShapeDtypeStruct((batch, 1, w), F32)],
        scratch_shapes=[pltpu.VMEM((SUBLANES + ts, w), F32), pltpu.VMEM((1, w), F32)],
        compiler_params=_cparams(("parallel", "arbitrary")),
        name="lru_prompt",
    )(xl, gl, conv0, h0.reshape(batch, 1, w), *lw)


def _lru_sample_kernel(xl_ref, gl_ref, c0_ref, c1_ref, c2_ref, h0_ref, cw_ref, cb_ref, wa_ref, ba_ref,
                       wx_ref, bx_ref, lam_ref, o_ref, hn_ref):
    cw = cw_ref[...]
    xl = xl_ref[...]
    xc = cb_ref[...] + c0_ref[...] * cw[0:1, :]
    xc = xc + c1_ref[...] * cw[1:2, :]
    xc = xc + c2_ref[...] * cw[2:3, :]
    xc = xc + xl * cw[3:4, :]
    a, u = _lru_gates(xc, wa_ref, ba_ref, wx_ref, bx_ref, lam_ref)
    h = a * h0_ref[...] + u
    hn_ref[...] = h
    o_ref[...] = h * _gelu_tanh(gl_ref[...])


def _lru_sample(xl, gl, state_conv, h0, lw):
    bd, w = xl.shape
    full = _full_spec((bd, w))
    return pl.pallas_call(
        _lru_sample_kernel,
        grid=(1,),
        in_specs=[full] * 6 + [_full_spec((CONV_WIDTH, w)), _full_spec((1, w)),
                               _full_spec((w, w)), _full_spec((1, w)),
                               _full_spec((w, w)), _full_spec((1, w)), _full_spec((1, w))],
        out_specs=[full, full],
        out_shape=[jax.ShapeDtypeStruct((bd, w), F32)] * 2,
        compiler_params=_cparams(("arbitrary",)),
        name="lru_sample",
    )(xl, gl, state_conv[:, 0], state_conv[:, 1], state_conv[:, 2], h0, *lw)


def _mix_peer_q_kernel(oa_ref, ol_ref, x_ref, ga_ref, gl_ref, wa_ref, wl_ref, gf_ref, wq_ref, k1_ref, k2_ref,
                       x1_ref, xn_ref, s1_ref, s2_ref, *, heads, half):
    na = _rms(oa_ref[...], ga_ref[...]).astype(BF16)
    nl = _rms(ol_ref[...], gl_ref[...]).astype(BF16)
    x1 = x_ref[...] + (_dot(na, wa_ref[...]) + _dot(nl, wl_ref[...]))
    x1_ref[...] = x1
    xn = _rms(x1, gf_ref[...]).astype(BF16)
    xn_ref[...] = xn
    qp = _dot(xn, wq_ref[...]).astype(BF16)
    k1 = k1_ref[...]
    k2 = k2_ref[...]
    for h in range(heads):
        base = h * 2 * half
        s1_ref[h] = _dot_nt(k1, qp[:, base:base + half])
        s2_ref[h] = _dot_nt(k2, qp[:, base + half:base + 2 * half])


def _mix_peer_q(o_att, o_lru, x2d, g_att, g_lru, w_out_bf, g_ffn, wq_bf, k1_bf, k2_bf, heads, tm):
    n, d = x2d.shape
    wa_w = o_att.shape[1]
    nk, half = k1_bf.shape
    kern = functools.partial(_mix_peer_q_kernel, heads=heads, half=half)
    tok = lambda w: pl.BlockSpec((tm, w), lambda i: (i, 0))
    sspec = pl.BlockSpec((heads, nk, tm), lambda i: (0, 0, i))
    return pl.pallas_call(
        kern,
        grid=(n // tm,),
        in_specs=[tok(wa_w), tok(o_lru.shape[1]), tok(d),
                  _full_spec((1, wa_w)), _full_spec((1, o_lru.shape[1])),
                  _full_spec((wa_w, d)), _full_spec((o_lru.shape[1], d)),
                  _full_spec((1, d)), _full_spec(wq_bf.shape),
                  _full_spec(k1_bf.shape), _full_spec(k2_bf.shape)],
        out_specs=[tok(d), tok(d), sspec, sspec],
        out_shape=[jax.ShapeDtypeStruct((n, d), F32), jax.ShapeDtypeStruct((n, d), BF16),
                   jax.ShapeDtypeStruct((heads, nk, n), F32), jax.ShapeDtypeStruct((heads, nk, n), F32)],
        compiler_params=_cparams(("parallel",)),
        name="mix_peer_q",
    )(o_att, o_lru, x2d, g_att.reshape(1, -1), g_lru.reshape(1, -1),
      w_out_bf[:wa_w], w_out_bf[wa_w:], g_ffn.reshape(1, d), wq_bf, k1_bf, k2_bf)


_CAND = [(i, j) for i in range(PEER_TOPK) for j in range(PEER_TOPK) if (i + 1) * (j + 1) <= PEER_TOPK]
_CAND_ROWS = -(-len(_CAND) // SUBLANES) * SUBLANES


def _oe_merge(lo, hi, r):
    step = r * 2
    if step < hi - lo:
        yield from _oe_merge(lo, hi, step)
        yield from _oe_merge(lo + r, hi, step)
        yield from [(i, i + r) for i in range(lo + r, hi - r, step)]
    else:
        yield (lo, lo + r)


def _oe_sort(lo, hi):
    if hi - lo >= 1:
        mid = lo + (hi - lo) // 2
        yield from _oe_sort(lo, mid)
        yield from _oe_sort(mid + 1, hi)
        yield from _oe_merge(lo, hi, 1)


def _cmp_exchange(vals, i, j):
    a, b = vals[i], vals[j]
    if b is None:
        return
    if a is None:
        vals[i], vals[j] = b, None
        return
    vals[i], vals[j] = jnp.maximum(a, b), jnp.minimum(a, b)


def _top_sorted(s, k):
    assert k == PEER_TOPK and s.shape[0] % SUBLANES == 0
    nslab = s.shape[0] // SUBLANES
    n = 1 << (nslab - 1).bit_length()
    assert n <= k
    vals = [s[i * SUBLANES:(i + 1) * SUBLANES, :] for i in range(nslab)] + [None] * (k - nslab)
    for i, j in _oe_sort(0, n - 1):
        _cmp_exchange(vals, i, j)
    for shift in (4, 2, 1):
        other = [None if v is None else pltpu.roll(v, shift, 0) for v in vals]
        merged = []
        for i in range(k):
            a, b = vals[i], other[k - 1 - i]
            merged.append(b if a is None else a if b is None else jnp.maximum(a, b))
        vals = merged
        dist = k // 2
        while dist >= 1:
            for i in range(k):
                if i & dist == 0:
                    _cmp_exchange(vals, i, i + dist)
            dist //= 2
    return vals


def _peer_select_kernel(s1_ref, s2_ref, ci_ref, cj_ref, a1_ref, a2_ref, tau_ref):
    s1 = s1_ref[...]
    s2 = s2_ref[...]
    t = s1.shape[1]
    v1 = [v[0:1, :] for v in _top_sorted(s1, PEER_TOPK)]
    v2 = [v[0:1, :] for v in _top_sorted(s2, PEER_TOPK)]
    reps = t // LANES
    ci = jnp.concatenate([ci_ref[...]] * reps, axis=1) if reps > 1 else ci_ref[...]
    cj = jnp.concatenate([cj_ref[...]] * reps, axis=1) if reps > 1 else cj_ref[...]
    b1 = jnp.zeros((_CAND_ROWS, t), F32)
    b2 = jnp.zeros((_CAND_ROWS, t), F32)
    for r in range(PEER_TOPK):
        b1 = jnp.where(ci == r, jnp.exp2((v1[r] - v1[0]) * LOG2E), b1)
        b2 = jnp.where(cj == r, jnp.exp2((v2[r] - v2[0]) * LOG2E), b2)
    top = _top_sorted(b1 * b2, PEER_TOPK)
    zsum = top[0]
    for r in range(1, PEER_TOPK):
        zsum = zsum + top[r]
    scale = 0.5 / zsum[0:1, :]
    tau_ref[...] = _top_sorted((b1 * scale) * b2, PEER_TOPK)[PEER_TOPK - 1][0:1, :]
    a1_ref[...] = jnp.exp2((s1 - v1[0]) * LOG2E) * scale
    a2_ref[...] = jnp.exp2((s2 - v2[0]) * LOG2E)


def _peer_select(s1, s2, ts):
    heads, nk, n = s1.shape
    ci = np.full((_CAND_ROWS, LANES), -1, np.int32)
    cj = np.full((_CAND_ROWS, LANES), -1, np.int32)
    for r, (i, j) in enumerate(_CAND):
        ci[r, :] = i
        cj[r, :] = j
    tab = pl.BlockSpec((None, nk, ts), lambda t, h: (h, 0, t))
    return pl.pallas_call(
        _peer_select_kernel,
        grid=(n // ts, heads),
        in_specs=[tab, tab, _full_spec(ci.shape), _full_spec(cj.shape)],
        out_specs=[tab, tab, pl.BlockSpec((None, 1, ts), lambda t, h: (h, 0, t))],
        out_shape=[jax.ShapeDtypeStruct((heads, nk, n), F32), jax.ShapeDtypeStruct((heads, nk, n), F32),
                   jax.ShapeDtypeStruct((heads, 1, n), F32)],
        compiler_params=_cparams(("parallel", "parallel")),
        name="peer_select",
    )(s1, s2, jnp.asarray(ci), jnp.asarray(cj))


def _peer_dense_kernel(xn_ref, x1_ref, a1_ref, a2_ref, tau_ref, u_ref, vt_ref, x2_ref,
                       acc_ref, h_ref, *, heads, nk, ec):
    c = pl.program_id(1)
    tb = xn_ref.shape[0]
    n_e1 = ec // nk
    c_in = math.sqrt(2.0 / math.pi)

    @pl.when(c == 0)
    def _():
        acc_ref[...] = jnp.zeros_like(acc_ref)

    parts = 4
    e_rows = ec // parts
    r_per = n_e1 // parts

    def act_part(q):
        return _dot_nt(u_ref[q * e_rows:(q + 1) * e_rows, :], xn_ref[...])

    def gate_part(q, act):
        for r in range(q * r_per, (q + 1) * r_per):
            for lt in range(tb // LANES):
                ls = slice(lt * LANES, (lt + 1) * LANES)
                tau = [jnp.broadcast_to(tau_ref[h, :, ls], (SUBLANES, LANES)) for h in range(heads)]
                row = [jnp.broadcast_to(a1_ref[h, r:r + 1, ls], (SUBLANES, LANES)) for h in range(heads)]
                for sb in range(0, nk, 2 * SUBLANES):
                    halves = []
                    for s0 in (sb, sb + SUBLANES):
                        gate = None
                        for h in range(heads):
                            e = a2_ref[h, s0:s0 + SUBLANES, ls] * row[h]
                            kept = jnp.where(e >= tau[h], e, 0.0)
                            gate = kept if gate is None else gate + kept
                        lo = (r - q * r_per) * nk + s0
                        x = act[lo:lo + SUBLANES, ls]
                        inner = x * ((x * x) * (0.044715 * c_in) + c_in)
                        halves.append(gate * (x * (1.0 + jnp.tanh(inner))))
                    h_ref[r * nk + sb:r * nk + sb + 2 * SUBLANES, ls] = (
                        jnp.concatenate(halves, axis=0).astype(BF16))

    acts = [act_part(0), act_part(1)]
    for q in range(parts):
        gate_part(q, acts[q])
        if q + 2 < parts:
            acts.append(act_part(q + 2))
    acc_ref[...] += _dot(vt_ref[...], h_ref[...])

    @pl.when(c == pl.num_programs(1) - 1)
    def _():
        x2_ref[...] = x1_ref[...] + acc_ref[...].T


def _peer_dense(xn_bf, x1, a1, a2, tau, u_bf, vt_bf, tb, ec):
    n, d = x1.shape
    heads, nk, _ = a1.shape
    n_exp = u_bf.shape[0]
    n_e1 = ec // nk
    assert n_e1 % SUBLANES == 0 and n_exp % ec == 0 and n % tb == 0
    kern = functools.partial(_peer_dense_kernel, heads=heads, nk=nk, ec=ec)
    return pl.pallas_call(
        kern,
        grid=(n // tb, n_exp // ec),
        in_specs=[pl.BlockSpec((tb, d), lambda i, c: (i, 0)),
                  pl.BlockSpec((tb, d), lambda i, c: (i, 0)),
                  pl.BlockSpec((heads, n_e1, tb), lambda i, c: (0, c, i)),
                  pl.BlockSpec((heads, nk, tb), lambda i, c: (0, 0, i)),
                  pl.BlockSpec((heads, 1, tb), lambda i, c: (0, 0, i)),
                  pl.BlockSpec((ec, d), lambda i, c: (c, 0)),
                  pl.BlockSpec((d, ec), lambda i, c: (0, c))],
        out_specs=pl.BlockSpec((tb, d), lambda i, c: (i, 0)),
        out_shape=jax.ShapeDtypeStruct((n, d), F32),
        scratch_shapes=[pltpu.VMEM((d, tb), F32), pltpu.VMEM((ec, tb), BF16)],
        compiler_params=_cparams(("parallel", "arbitrary")),
        name="peer_dense",
    )(xn_bf, x1, a1, a2, tau, u_bf, vt_bf)


def _ple_final_kernel(x_ref, p_ref, gp_ref, wg_ref, wp_ref, gf_ref, y_ref):
    x = x_ref[...]
    gate = jax.nn.sigmoid(_dot(_rms(x, gp_ref[...]).astype(BF16), wg_ref[...]))
    x3 = x + gate * _dot(p_ref[...].astype(BF16), wp_ref[...])
    y_ref[...] = _rms(x3, gf_ref[...])


def _ple_final(x2, p2d, g_ple, wg_bf, wp_bf, g_final, tm):
    n, d = x2.shape
    pd = p2d.shape[1]
    return pl.pallas_call(
        _ple_final_kernel,
        grid=(n // tm,),
        in_specs=[pl.BlockSpec((tm, d), lambda i: (i, 0)),
                  pl.BlockSpec((tm, pd), lambda i: (i, 0)),
                  _full_spec((1, d)), _full_spec((d, d)), _full_spec((pd, d)), _full_spec((1, d))],
        out_specs=pl.BlockSpec((tm, d), lambda i: (i, 0)),
        out_shape=jax.ShapeDtypeStruct((n, d), F32),
        compiler_params=_cparams(("parallel",)),
        name="ple_final",
    )(x2, p2d, g_ple.reshape(1, d), wg_bf, wp_bf, g_final.reshape(1, d))


def _pick(n, pref):
    t = min(n, pref)
    while n % t:
        t //= 2
    return t


def _tiles(n_tokens, seq):
    return dict(
        tm=_pick(n_tokens, 512),
        tq=_pick(seq, 256),
        ts_lru=_pick(seq, 256),
        ts_sel=_pick(n_tokens, 512),
        tb=_pick(n_tokens, 512),
        ec=2048,
    )


def _tail(x1, xn_bf, s1, s2, p2d, wts, tiles):
    a1, a2, tau = _peer_select(s1, s2, tiles["ts_sel"])
    x2 = _peer_dense(xn_bf, x1, a1, a2, tau, wts["u"], wts["vt"], tiles["tb"], tiles["ec"])
    return _ple_final(x2, p2d, wts["g_ple"], wts["wg"], wts["wp"], wts["g_final"], tiles["tm"])


def kernel(x_prompt, x_sample, cache_k, cache_v, state_conv, state_h, page_table, p_prompt, p_sample, norm_mix, w_in, sb_bias, conv_w, conv_b, gate_a_w, gate_a_b, gate_x_w, gate_x_b, lru_lambda, norm_attn_out, norm_lru_out, w_out, norm_ffn, peer_w_q, peer_keys1, peer_keys2, peer_u, peer_v, norm_ple, ple_w_gate, ple_w_proj, norm_final):
    depth = w_in.shape[0]
    assert depth == 1, "single-layer trunk"
    bp, seq, d = x_prompt.shape
    bd, dseq, _ = x_sample.shape
    assert dseq == 1, "one new token per sampled sequence"
    heads = sb_bias.shape[1]
    head_dim = cache_k.shape[-1]
    sbw = heads * head_dim
    lruw = lru_lambda.shape[1]
    peer_heads = peer_w_q.shape[2] // (2 * peer_keys1.shape[2])

    li = 0
    w_in_bf = w_in[li].astype(BF16)
    lw = _lru_weights(conv_w[li], conv_b[li], gate_a_w[li], gate_a_b[li], gate_x_w[li], gate_x_b[li],
                      lru_lambda[li])
    w_out_bf = w_out[li].astype(BF16)
    wq_bf = peer_w_q[li].astype(BF16)
    k1_bf = peer_keys1[li].astype(BF16)
    k2_bf = peer_keys2[li].astype(BF16)
    wts = dict(u=peer_u[li].astype(BF16), vt=peer_v[li].T.astype(BF16), g_ple=norm_ple[li],
               wg=ple_w_gate[li].astype(BF16), wp=ple_w_proj[li].astype(BF16), g_final=norm_final)

    n_p = bp * seq
    tp = _tiles(n_p, seq)
    xp2d = x_prompt.reshape(n_p, d)
    q, k_fm, v_fm, xl, gl = _in_proj(xp2d, norm_mix[li], w_in_bf, tp["tm"], seq=seq)
    o_att = _sb_prompt(q, k_fm, v_fm, sb_bias[li], bp, seq, head_dim, tp["tq"])
    conv0 = jnp.zeros((bp, CONV_WIDTH - 1, lruw), F32)
    h0 = jnp.zeros((bp, lruw), F32)
    o_lru, conv_p, h_p = _lru_prompt(xl, gl, conv0, h0, lw, bp, seq, tp["ts_lru"])
    x1, xn_bf, s1, s2 = _mix_peer_q(o_att, o_lru, xp2d, norm_attn_out[li], norm_lru_out[li], w_out_bf,
                                    norm_ffn[li], wq_bf, k1_bf, k2_bf, peer_heads, tp["tm"])
    y_p = _tail(x1, xn_bf, s1, s2, p_prompt[li].reshape(n_p, -1), wts, tp)

    ts_ = _tiles(bd, 1)
    xs2d = x_sample.reshape(bd, d)
    qs, ks, vs, xls, gls = _in_proj(xs2d, norm_mix[li], w_in_bf, ts_["tm"])
    o_att_s = _sb_sample(qs, sb_bias[li], cache_k[li], cache_v[li], page_table, heads, head_dim,
                         pp=math.gcd(page_table.shape[1], 16))
    o_lru_s, h_s = _lru_sample(xls, gls, state_conv[li], state_h[li], lw)
    conv_s = jnp.concatenate([state_conv[li][:, 1:], xls[:, None, :]], axis=1)
    x1s, xns_bf, s1s, s2s = _mix_peer_q(o_att_s, o_lru_s, xs2d, norm_attn_out[li], norm_lru_out[li], w_out_bf,
                                        norm_ffn[li], wq_bf, k1_bf, k2_bf, peer_heads, ts_["tm"])
    y_s = _tail(x1s, xns_bf, s1s, s2s, p_sample[li].reshape(bd, -1), wts, ts_)

    return (y_p.reshape(bp, seq, d), y_s.reshape(bd, 1, d),
            jnp.transpose(k_fm.reshape(bp, heads, head_dim, seq), (0, 3, 1, 2))[None],
            jnp.transpose(v_fm.reshape(bp, heads, head_dim, seq), (0, 3, 1, 2))[None],
            conv_p[None], h_p.reshape(1, bp, lruw),
            ks.reshape(1, bd, 1, heads, head_dim), vs.reshape(1, bd, 1, heads, head_dim),
            conv_s[None], h_s[None])
```

```python
import functools
import math

import numpy as np
import jax
import jax.numpy as jnp
from jax import lax
from jax.experimental import pallas as pl
from jax.experimental.pallas import tpu as pltpu

F32 = jnp.float32
BF16 = jnp.bfloat16

RMS_EPS = 1e-6
LRU_C = 8.0
CONV_WIDTH = 4
PEER_TOPK = 16
LOG2E = 1.4426950408889634
NEG_INF = float("-inf")

LANES = 128
SUBLANES = 8
VMEM_LIMIT = 56 * 1024 * 1024


def _cparams(sem, flags=None):
    return pltpu.CompilerParams(dimension_semantics=sem, vmem_limit_bytes=VMEM_LIMIT, flags=flags)


def _rms(x, g):
    ms = jnp.mean(x * x, axis=-1, keepdims=True)
    return x * lax.rsqrt(ms + RMS_EPS) * g


def _softplus(z):
    return jnp.maximum(z, 0.0) + jnp.log1p(jnp.exp(-jnp.abs(z)))


def _log2_keep(z2):
    nz = -z2
    return jnp.minimum(nz, 0.0) - jnp.log2(1.0 + jnp.exp2(jnp.minimum(z2, nz)))


def _gelu_tanh(x):
    c = math.sqrt(2.0 / math.pi)
    return 0.5 * x * (1.0 + jnp.tanh(c * (x + 0.044715 * (x * x * x))))


def _dot(a, b):
    return jnp.dot(a, b, preferred_element_type=F32)


def _dot_nt(a, b):
    return lax.dot_general(a, b, (((1,), (1,)), ((), ())), preferred_element_type=F32)


def _split_bf16(x):
    hi = x.astype(BF16)
    lo = (x - hi.astype(F32)).astype(BF16)
    return hi, lo


def _full_spec(shape):
    nd = len(shape)
    return pl.BlockSpec(shape, lambda *_: (0,) * nd)


def _in_proj_kernel(x_ref, g_ref, w_ref, q_ref, k_ref, v_ref, xl_ref, gl_ref, *, kv_feature_major):
    xn = _rms(x_ref[...], g_ref[...]).astype(BF16)
    proj = _dot(xn, w_ref[...])
    width = q_ref.shape[-1]
    for i, ref in enumerate((q_ref, k_ref, v_ref, xl_ref, gl_ref)):
        part = proj[:, i * width:(i + 1) * width]
        ref[...] = part.T if (kv_feature_major and i in (1, 2)) else part


def _in_proj(x2d, norm_g, w_in_bf, tm, seq=None):
    n, d = x2d.shape
    width = w_in_bf.shape[1] // 5
    out = jax.ShapeDtypeStruct((n, width), F32)
    tok = pl.BlockSpec((tm, width), lambda i: (i, 0))
    if seq is None:
        kv_spec, kv_shape = tok, out
    else:
        nblk = seq // tm
        kv_spec = pl.BlockSpec((None, width, tm), lambda i: (i // nblk, 0, i % nblk))
        kv_shape = jax.ShapeDtypeStruct((n // seq, width, seq), F32)
    return pl.pallas_call(
        functools.partial(_in_proj_kernel, kv_feature_major=seq is not None),
        grid=(n // tm,),
        in_specs=[pl.BlockSpec((tm, d), lambda i: (i, 0)),
                  _full_spec((1, d)),
                  _full_spec(w_in_bf.shape)],
        out_specs=[tok, kv_spec, kv_spec, tok, tok],
        out_shape=[out, kv_shape, kv_shape, out, out],
        compiler_params=_cparams(("parallel",)),
        name="in_proj",
    )(x2d, norm_g.reshape(1, d), w_in_bf)


def _sb_prompt_kernel(bias_ref, q_ref, k_ref, v_ref, tri_ref, o_ref, *, tq, head_dim):
    hp = pl.program_id(1)
    i = pl.program_id(2)
    nh = LANES // head_dim
    lane = lax.broadcasted_iota(jnp.int32, (1, LANES), 1)
    q = q_ref[...] * (head_dim ** -0.5 * LOG2E)
    tri = tri_ref[...]
    row = lax.broadcasted_iota(jnp.int32, (tq, tq), 0)
    col = lax.broadcasted_iota(jnp.int32, (tq, tq), 1)
    diag_valid = col < row
    in_head = [(lane >= hh * head_dim) & (lane < (hh + 1) * head_dim) for hh in range(nh)]
    bias = [bias_ref[hp * nh + hh] * LOG2E for hh in range(nh)]
    qh = [jnp.where(in_head[hh], q, 0.0).astype(BF16) for hh in range(nh)]

    def process(blocks, state):
        staged = []
        for j, masked in blocks:
            start = pl.multiple_of(j * tq, tq)
            kblk = k_ref[:, pl.ds(start, tq)].astype(BF16)
            vblk = v_ref[:, pl.ds(start, tq)].astype(BF16)
            per_head = []
            for hh in range(nh):
                z = _dot(qh[hh], kblk) + bias[hh]
                lk = _log2_keep(z)
                if masked:
                    lk = jnp.where(diag_valid, lk, 0.0)
                hi, lo = _split_bf16(lk)
                r_incl = _dot(hi, tri) + _dot(lo, tri)
                per_head.append((z, r_incl, jnp.sum(lk, axis=1, keepdims=True)))
            staged.append((vblk, masked, per_head))
        out_state = []
        for hh in range(nh):
            o_acc, carry = state[2 * hh], state[2 * hh + 1]
            for vblk, masked, per_head in staged:
                z, r_incl, tot = per_head[hh]
                w = jnp.exp2(z + r_incl + carry)
                if masked:
                    w = jnp.where(diag_valid, w, 0.0)
                o_acc = o_acc + _dot_nt(w.astype(BF16), vblk)
                carry = carry + tot
            out_state += [o_acc, carry]
        return tuple(out_state)

    state = (jnp.zeros((tq, LANES), F32), jnp.zeros((tq, 1), F32)) * nh
    state = process([(i, True)], state)
    state = lax.fori_loop(
        0, i // 2, lambda jj, st: process([(i - 1 - 2 * jj, False), (i - 2 - 2 * jj, False)], st), state)
    state = lax.fori_loop(0, i % 2, lambda jj, st: process([(0, False)], st), state)
    out = jnp.zeros((tq, LANES), F32)
    for hh in range(nh):
        out = jnp.where(in_head[hh], state[2 * hh], out)
    o_ref[...] = out


def _sb_prompt(q, k_fm, v_fm, sb_bias, batch, seq, head_dim, tq):
    n, width = q.shape
    nblk = seq // tq
    npair = width // LANES
    tri = jnp.asarray(np.tril(np.ones((tq, tq), np.float32)), BF16)
    kern = functools.partial(_sb_prompt_kernel, tq=tq, head_dim=head_dim)
    return pl.pallas_call(
        kern,
        grid=(batch, npair, nblk),
        in_specs=[pl.BlockSpec(memory_space=pltpu.SMEM),
                  pl.BlockSpec((tq, LANES), lambda b, h, i: (b * nblk + i, h)),
                  pl.BlockSpec((None, LANES, seq), lambda b, h, i: (b, h, 0)),
                  pl.BlockSpec((None, LANES, seq), lambda b, h, i: (b, h, 0)),
                  _full_spec((tq, tq))],
        out_specs=pl.BlockSpec((tq, LANES), lambda b, h, i: (b * nblk + i, h)),
        out_shape=jax.ShapeDtypeStruct((n, width), F32),
        compiler_params=_cparams(("parallel", "parallel", "arbitrary")),
        name="sb_prompt",
    )(sb_bias, q, k_fm, v_fm, tri)


def _sb_sample_kernel(pt_ref, bias_ref, qcol_ref, tri_ref, *refs, pp, heads, head_dim):
    k_refs = refs[:pp]
    v_refs = refs[pp:2 * pp]
    o_ref = refs[2 * pp]
    acc_ref, carry_ref = refs[2 * pp + 1:]
    j = pl.program_id(1)
    page = tri_ref.shape[0]
    assert heads == SUBLANES and head_dim % SUBLANES == 0
    nslab = head_dim // SUBLANES

    @pl.when(j == 0)
    def _():
        acc_ref[...] = jnp.zeros_like(acc_ref)
        carry_ref[...] = jnp.zeros_like(carry_ref)

    def slab(ref, h, r):
        lo = h * head_dim + r * SUBLANES
        return ref[lo:lo + SUBLANES, :]

    sub = lax.broadcasted_iota(jnp.int32, (SUBLANES, page), 0)
    zs = [jnp.zeros((SUBLANES, page), F32) for _ in range(pp)]
    for h in range(heads):
        qs = [slab(qcol_ref, h, r) for r in range(nslab)]
        for p in range(pp):
            s8 = slab(k_refs[p], h, 0) * qs[0]
            for r in range(1, nslab):
                s8 = s8 + slab(k_refs[p], h, r) * qs[r]
            for sh in (4, 2, 1):
                s8 = s8 + pltpu.roll(s8, sh, 0)
            zs[p] = jnp.where(sub == h, s8, zs[p])
    z = jnp.concatenate(zs, axis=0) + jnp.concatenate([bias_ref[...] * LOG2E] * pp, axis=0)
    lk = _log2_keep(z)
    hi, lo = _split_bf16(lk)
    tri = tri_ref[...]
    r_incl = _dot(hi, tri) + _dot(lo, tri)
    tot = jnp.sum(lk, axis=1, keepdims=True)
    carry = carry_ref[...]
    ws = []
    for p in range(pp):
        sl = slice(p * heads, (p + 1) * heads)
        ws.append(jnp.exp2(z[sl] + r_incl[sl] + carry))
        carry = carry + tot[sl]
    carry_ref[...] = carry
    for h in range(heads):
        acc = [slab(acc_ref, h, r) for r in range(nslab)]
        for p in range(pp):
            wb = jnp.broadcast_to(ws[p][h:h + 1, :], (SUBLANES, page))
            for r in range(nslab):
                acc[r] = acc[r] + slab(v_refs[p], h, r) * wb
        for r in range(nslab):
            lo_r = h * head_dim + r * SUBLANES
            acc_ref[lo_r:lo_r + SUBLANES, :] = acc[r]

    @pl.when(j == pl.num_programs(1) - 1)
    def _():
        o_ref[...] = jnp.sum(acc_ref[...], axis=1, keepdims=True)


def _sb_sample(q, sb_bias, cache_k, cache_v, page_table, heads, head_dim, pp):
    bd, width = q.shape
    n_phys, page = cache_k.shape[0], cache_k.shape[1]
    n_pages = page_table.shape[1]
    assert n_pages % pp == 0
    ck = jnp.transpose(cache_k, (0, 2, 3, 1)).reshape(n_phys, width, page)
    cv = jnp.transpose(cache_v, (0, 2, 3, 1)).reshape(n_phys, width, page)
    qcol = jnp.broadcast_to((q * (head_dim ** -0.5 * LOG2E))[:, :, None], (bd, width, page))
    tri = jnp.asarray(np.tril(np.ones((page, page), np.float32)), BF16)
    kern = functools.partial(_sb_sample_kernel, pp=pp, heads=heads, head_dim=head_dim)

    def page_map(p):
        return lambda b, j, pt: (pt[b, n_pages - 1 - (j * pp + p)], 0, 0)

    kv_specs = [pl.BlockSpec((None, width, page), page_map(p)) for p in range(pp)]
    grid_spec = pltpu.PrefetchScalarGridSpec(
        num_scalar_prefetch=1,
        grid=(bd, n_pages // pp),
        in_specs=[pl.BlockSpec((heads, 1), lambda b, j, pt: (0, 0)),
                  pl.BlockSpec((None, width, page), lambda b, j, pt: (b, 0, 0)),
                  pl.BlockSpec((page, page), lambda b, j, pt: (0, 0))] + kv_specs + kv_specs,
        out_specs=pl.BlockSpec((None, width, 1), lambda b, j, pt: (b, 0, 0)),
        scratch_shapes=[pltpu.VMEM((width, page), F32),
                        pltpu.VMEM((heads, 1), F32)],
    )
    out = pl.pallas_call(
        kern,
        grid_spec=grid_spec,
        out_shape=jax.ShapeDtypeStruct((bd, width, 1), F32),
        compiler_params=_cparams(("parallel", "arbitrary")),
        name="sb_sample",
    )(page_table, sb_bias.reshape(heads, 1), qcol, tri, *([ck] * pp), *([cv] * pp))
    return out.reshape(bd, width)


def _lru_gates(xc, wa_ref, ba_ref, wx_ref, bx_ref, lam_ref):
    xcb = xc.astype(BF16)
    r = jax.nn.sigmoid(_dot(xcb, wa_ref[...]) + ba_ref[...])
    ig = jax.nn.sigmoid(_dot(xcb, wx_ref[...]) + bx_ref[...])
    log_a = (-LRU_C) * r * _softplus(-lam_ref[...])
    a = jnp.exp(log_a)
    one_minus_a2 = -jnp.tanh(log_a) * (a * a + 1.0)
    u = jnp.sqrt(one_minus_a2) * (ig * xc)
    return a, u


def _lru_prompt_kernel(xl_ref, gl_ref, conv0_ref, h0_ref, cw_ref, cb_ref, wa_ref, ba_ref, wx_ref, bx_ref,
                       lam_ref, o_ref, convn_ref, hn_ref, xpad_ref, h_ref, *, ts):
    tb = pl.program_id(1)
    tail = CONV_WIDTH - 1

    @pl.when(tb == 0)
    def _():
        xpad_ref[SUBLANES - tail:SUBLANES, :] = conv0_ref[...]
        h_ref[...] = h0_ref[...]

    xl = xl_ref[...]
    xpad_ref[SUBLANES:SUBLANES + ts, :] = xl
    cw = cw_ref[...]
    xc = cb_ref[...]
    for jw in range(CONV_WIDTH):
        off = SUBLANES - tail + jw
        xc = xc + xpad_ref[off:off + ts, :] * cw[jw:jw + 1, :]
    a, u = _lru_gates(xc, wa_ref, ba_ref, wx_ref, bx_ref, lam_ref)
    rows = lax.broadcasted_iota(jnp.int32, a.shape, 0)
    d = 1
    while d < ts:
        keep = rows >= d
        a_sh = jnp.where(keep, pltpu.roll(a, d, 0), 1.0)
        u_sh = jnp.where(keep, pltpu.roll(u, d, 0), 0.0)
        u = u + a * u_sh
        a = a * a_sh
        d *= 2
    hs = a * h_ref[...] + u
    o_ref[...] = hs * _gelu_tanh(gl_ref[...])
    h_ref[...] = hs[ts - 1:ts, :]
    xpad_ref[SUBLANES - tail:SUBLANES, :] = xl[ts - tail:ts, :]

    @pl.when(tb == pl.num_programs(1) - 1)
    def _():
        convn_ref[...] = xl[ts - tail:ts, :]
        hn_ref[...] = hs[ts - 1:ts, :]


def _lru_weights(conv_w, conv_b, gate_a_w, gate_a_b, gate_x_w, gate_x_b, lru_lambda):
    nb, bd, _ = gate_a_w.shape
    w = nb * bd
    eye = jnp.eye(nb, dtype=F32)
    wa = jnp.einsum('nde,nm->ndme', gate_a_w, eye).reshape(w, w).astype(BF16)
    wx = jnp.einsum('nde,nm->ndme', gate_x_w, eye).reshape(w, w).astype(BF16)
    return (conv_w, conv_b.reshape(1, w), wa, gate_a_b.reshape(1, w), wx, gate_x_b.reshape(1, w),
            lru_lambda.reshape(1, w))


def _lru_prompt(xl, gl, conv0, h0, lw, batch, seq, ts):
    n, w = xl.shape
    nblk = seq // ts
    tail = CONV_WIDTH - 1
    kern = functools.partial(_lru_prompt_kernel, ts=ts)
    tok = pl.BlockSpec((ts, w), lambda b, t: (b * nblk + t, 0))
    return pl.pallas_call(
        kern,
        grid=(batch, nblk),
        in_specs=[tok, tok,
                  pl.BlockSpec((None, tail, w), lambda b, t: (b, 0, 0)),
                  pl.BlockSpec((None, 1, w), lambda b, t: (b, 0, 0)),
                  _full_spec((CONV_WIDTH, w)), _full_spec((1, w)),
                  _full_spec((w, w)), _full_spec((1, w)),
                  _full_spec((w, w)), _full_spec((1, w)), _full_spec((1, w))],
        out_specs=[tok,
                   pl.BlockSpec((None, tail, w), lambda b, t: (b, 0, 0)),
                   pl.BlockSpec((None, 1, w), lambda b, t: (b, 0, 0))],
        out_shape=[jax.ShapeDtypeStruct((n, w), F32),
                   jax.ShapeDtypeStruct((batch, tail, w), F32),
                   jax.ShapeDtypeStruct((batch, 1, w), F32)],
        scratch_shapes=[pltpu.VMEM((SUBLANES + ts, w), F32), pltpu.VMEM((1, w), F32)],
        compiler_params=_cparams(("parallel", "arbitrary")),
        name="lru_prompt",
    )(xl, gl, conv0, h0.reshape(batch, 1, w), *lw)


def _lru_sample_kernel(xl_ref, gl_ref, c0_ref, c1_ref, c2_ref, h0_ref, cw_ref, cb_ref, wa_ref, ba_ref,
                       wx_ref, bx_ref, lam_ref, o_ref, hn_ref):
    cw = cw_ref[...]
    xl = xl_ref[...]
    xc = cb_ref[...] + c0_ref[...] * cw[0:1, :]
    xc = xc + c1_ref[...] * cw[1:2, :]
    xc = xc + c2_ref[...] * cw[2:3, :]
    xc = xc + xl * cw[3:4, :]
    a, u = _lru_gates(xc, wa_ref, ba_ref, wx_ref, bx_ref, lam_ref)
    h = a * h0_ref[...] + u
    hn_ref[...] = h
    o_ref[...] = h * _gelu_tanh(gl_ref[...])


def _lru_sample(xl, gl, state_conv, h0, lw):
    bd, w = xl.shape
    full = _full_spec((bd, w))
    return pl.pallas_call(
        _lru_sample_kernel,
        grid=(1,),
        in_specs=[full] * 6 + [_full_spec((CONV_WIDTH, w)), _full_spec((1, w)),
                               _full_spec((w, w)), _full_spec((1, w)),
                               _full_spec((w, w)), _full_spec((1, w)), _full_spec((1, w))],
        out_specs=[full, full],
        out_shape=[jax.ShapeDtypeStruct((bd, w), F32)] * 2,
        compiler_params=_cparams(("arbitrary",)),
        name="lru_sample",
    )(xl, gl, state_conv[:, 0], state_conv[:, 1], state_conv[:, 2], h0, *lw)


def _mix_peer_q_kernel(oa_ref, ol_ref, x_ref, ga_ref, gl_ref, wa_ref, wl_ref, gf_ref, wq_ref, k1_ref, k2_ref,
                       x1_ref, xn_ref, s1_ref, s2_ref, *, heads, half):
    na = _rms(oa_ref[...], ga_ref[...]).astype(BF16)
    nl = _rms(ol_ref[...], gl_ref[...]).astype(BF16)
    x1 = x_ref[...] + (_dot(na, wa_ref[...]) + _dot(nl, wl_ref[...]))
    x1_ref[...] = x1
    xn = _rms(x1, gf_ref[...]).astype(BF16)
    xn_ref[...] = xn
    qp = _dot(xn, wq_ref[...]).astype(BF16)
    k1 = k1_ref[...]
    k2 = k2_ref[...]
    for h in range(heads):
        base = h * 2 * half
        s1_ref[h] = _dot_nt(k1, qp[:, base:base + half])
        s2_ref[h] = _dot_nt(k2, qp[:, base + half:base + 2 * half])


def _mix_peer_q(o_att, o_lru, x2d, g_att, g_lru, w_out_bf, g_ffn, wq_bf, k1_bf, k2_bf, heads, tm):
    n, d = x2d.shape
    wa_w = o_att.shape[1]
    nk, half = k1_bf.shape
    kern = functools.partial(_mix_peer_q_kernel, heads=heads, half=half)
    tok = lambda w: pl.BlockSpec((tm, w), lambda i: (i, 0))
    sspec = pl.BlockSpec((heads, nk, tm), lambda i: (0, 0, i))
    return pl.pallas_call(
        kern,
        grid=(n // tm,),
        in_specs=[tok(wa_w), tok(o_lru.shape[1]), tok(d),
                  _full_spec((1, wa_w)), _full_spec((1, o_lru.shape[1])),
                  _full_spec((wa_w, d)), _full_spec((o_lru.shape[1], d)),
                  _full_spec((1, d)), _full_spec(wq_bf.shape),
                  _full_spec(k1_bf.shape), _full_spec(k2_bf.shape)],
        out_specs=[tok(d), tok(d), sspec, sspec],
        out_shape=[jax.ShapeDtypeStruct((n, d), F32), jax.ShapeDtypeStruct((n, d), BF16),
                   jax.ShapeDtypeStruct((heads, nk, n), F32), jax.ShapeDtypeStruct((heads, nk, n), F32)],
        compiler_params=_cparams(("parallel",)),
        name="mix_peer_q",
    )(o_att, o_lru, x2d, g_att.reshape(1, -1), g_lru.reshape(1, -1),
      w_out_bf[:wa_w], w_out_bf[wa_w:], g_ffn.reshape(1, d), wq_bf, k1_bf, k2_bf)


_CAND = [(i, j) for i in range(PEER_TOPK) for j in range(PEER_TOPK) if (i + 1) * (j + 1) <= PEER_TOPK]
_CAND_ROWS = -(-len(_CAND) // SUBLANES) * SUBLANES


def _oe_merge(lo, hi, r):
    step = r * 2
    if step < hi - lo:
        yield from _oe_merge(lo, hi, step)
        yield from _oe_merge(lo + r, hi, step)
        yield from [(i, i + r) for i in range(lo + r, hi - r, step)]
    else:
        yield (lo, lo + r)


def _oe_sort(lo, hi):
    if hi - lo >= 1:
        mid = lo + (hi - lo) // 2
        yield from _oe_sort(lo, mid)
        yield from _oe_sort(mid + 1, hi)
        yield from _oe_merge(lo, hi, 1)


def _cmp_exchange(vals, i, j):
    a, b = vals[i], vals[j]
    if b is None:
        return
    if a is None:
        vals[i], vals[j] = b, None
        return
    vals[i], vals[j] = jnp.maximum(a, b), jnp.minimum(a, b)


def _top_sorted(s, k):
    assert k == PEER_TOPK and s.shape[0] % SUBLANES == 0
    nslab = s.shape[0] // SUBLANES
    n = 1 << (nslab - 1).bit_length()
    assert n <= k
    vals = [s[i * SUBLANES:(i + 1) * SUBLANES, :] for i in range(nslab)] + [None] * (k - nslab)
    for i, j in _oe_sort(0, n - 1):
        _cmp_exchange(vals, i, j)
    for shift in (4, 2, 1):
        other = [None if v is None else pltpu.roll(v, shift, 0) for v in vals]
        merged = []
        for i in range(k):
            a, b = vals[i], other[k - 1 - i]
            merged.append(b if a is None else a if b is None else jnp.maximum(a, b))
        vals = merged
        dist = k // 2
        while dist >= 1:
            for i in range(k):
                if i & dist == 0:
                    _cmp_exchange(vals, i, i + dist)
            dist //= 2
    return vals


def _peer_select_kernel(s1_ref, s2_ref, ci_ref, cj_ref, a1_ref, a2_ref, tau_ref):
    s1 = s1_ref[...]
    s2 = s2_ref[...]
    t = s1.shape[1]
    v1 = [v[0:1, :] for v in _top_sorted(s1, PEER_TOPK)]
    v2 = [v[0:1, :] for v in _top_sorted(s2, PEER_TOPK)]
    reps = t // LANES
    ci = jnp.concatenate([ci_ref[...]] * reps, axis=1) if reps > 1 else ci_ref[...]
    cj = jnp.concatenate([cj_ref[...]] * reps, axis=1) if reps > 1 else cj_ref[...]
    b1 = jnp.zeros((_CAND_ROWS, t), F32)
    b2 = jnp.zeros((_CAND_ROWS, t), F32)
    for r in range(PEER_TOPK):
        b1 = jnp.where(ci == r, jnp.exp2((v1[r] - v1[0]) * LOG2E), b1)
        b2 = jnp.where(cj == r, jnp.exp2((v2[r] - v2[0]) * LOG2E), b2)
    top = _top_sorted(b1 * b2, PEER_TOPK)
    zsum = top[0]
    for r in range(1, PEER_TOPK):
        zsum = zsum + top[r]
    scale = 0.5 / zsum[0:1, :]
    tau_ref[...] = _top_sorted((b1 * scale) * b2, PEER_TOPK)[PEER_TOPK - 1][0:1, :]
    a1_ref[...] = jnp.exp2((s1 - v1[0]) * LOG2E) * scale
    a2_ref[...] = jnp.exp2((s2 - v2[0]) * LOG2E)


def _peer_select(s1, s2, ts):
    heads, nk, n = s1.shape
    ci = np.full((_CAND_ROWS, LANES), -1, np.int32)
    cj = np.full((_CAND_ROWS, LANES), -1, np.int32)
    for r, (i, j) in enumerate(_CAND):
        ci[r, :] = i
        cj[r, :] = j
    tab = pl.BlockSpec((None, nk, ts), lambda t, h: (h, 0, t))
    return pl.pallas_call(
        _peer_select_kernel,
        grid=(n // ts, heads),
        in_specs=[tab, tab, _full_spec(ci.shape), _full_spec(cj.shape)],
        out_specs=[tab, tab, pl.BlockSpec((None, 1, ts), lambda t, h: (h, 0, t))],
        out_shape=[jax.ShapeDtypeStruct((heads, nk, n), F32), jax.ShapeDtypeStruct((heads, nk, n), F32),
                   jax.ShapeDtypeStruct((heads, 1, n), F32)],
        compiler_params=_cparams(("parallel", "parallel")),
        name="peer_select",
    )(s1, s2, jnp.asarray(ci), jnp.asarray(cj))


def _peer_dense_kernel(xn_ref, x1_ref, a1_ref, a2_ref, tau_ref, u_ref, vt_ref, x2_ref,
                       acc_ref, h_ref, *, heads, nk, ec):
    c = pl.program_id(1)
    tb = xn_ref.shape[0]
    n_e1 = ec // nk
    c_in = math.sqrt(2.0 / math.pi)

    @pl.when(c == 0)
    def _():
        acc_ref[...] = jnp.zeros_like(acc_ref)

    parts = 4
    e_rows = ec // parts
    r_per = n_e1 // parts

    def act_part(q):
        return _dot_nt(u_ref[q * e_rows:(q + 1) * e_rows, :], xn_ref[...])

    def gate_part(q, act):
        for r in range(q * r_per, (q + 1) * r_per):
            for lt in range(tb // LANES):
                ls = slice(lt * LANES, (lt + 1) * LANES)
                tau = [jnp.broadcast_to(tau_ref[h, :, ls], (SUBLANES, LANES)) for h in range(heads)]
                row = [jnp.broadcast_to(a1_ref[h, r:r + 1, ls], (SUBLANES, LANES)) for h in range(heads)]
                for sb in range(0, nk, 2 * SUBLANES):
                    halves = []
                    for s0 in (sb, sb + SUBLANES):
                        gate = None
                        for h in range(heads):
                            e = a2_ref[h, s0:s0 + SUBLANES, ls] * row[h]
                            kept = jnp.where(e >= tau[h], e, 0.0)
                            gate = kept if gate is None else gate + kept
                        lo = (r - q * r_per) * nk + s0
                        x = act[lo:lo + SUBLANES, ls]
                        inner = x * ((x * x) * (0.044715 * c_in) + c_in)
                        halves.append(gate * (x * (1.0 + jnp.tanh(inner))))
                    h_ref[r * nk + sb:r * nk + sb + 2 * SUBLANES, ls] = (
                        jnp.concatenate(halves, axis=0).astype(BF16))

    acts = [act_part(0), act_part(1)]
    for q in range(parts):
        gate_part(q, acts[q])
        if q + 2 < parts:
            acts.append(act_part(q + 2))
    acc_ref[...] += _dot(vt_ref[...], h_ref[...])

    @pl.when(c == pl.num_programs(1) - 1)
    def _():
        x2_ref[...] = x1_ref[...] + acc_ref[...].T


def _peer_dense(xn_bf, x1, a1, a2, tau, u_bf, vt_bf, tb, ec):
    n, d = x1.shape
    heads, nk, _ = a1.shape
    n_exp = u_bf.shape[0]
    n_e1 = ec // nk
    assert n_e1 % SUBLANES == 0 and n_exp % ec == 0 and n % tb == 0
    kern = functools.partial(_peer_dense_kernel, heads=heads, nk=nk, ec=ec)
    return pl.pallas_call(
        kern,
        grid=(n // tb, n_exp // ec),
        in_specs=[pl.BlockSpec((tb, d), lambda i, c: (i, 0)),
                  pl.BlockSpec((tb, d), lambda i, c: (i, 0)),
                  pl.BlockSpec((heads, n_e1, tb), lambda i, c: (0, c, i)),
                  pl.BlockSpec((heads, nk, tb), lambda i, c: (0, 0, i)),
                  pl.BlockSpec((heads, 1, tb), lambda i, c: (0, 0, i)),
                  pl.BlockSpec((ec, d), lambda i, c: (c, 0)),
                  pl.BlockSpec((d, ec), lambda i, c: (0, c))],
        out_specs=pl.BlockSpec((tb, d), lambda i, c: (i, 0)),
        out_shape=jax.ShapeDtypeStruct((n, d), F32),
        scratch_shapes=[pltpu.VMEM((d, tb), F32), pltpu.VMEM((ec, tb), BF16)],
        compiler_params=_cparams(("parallel", "arbitrary")),
        name="peer_dense",
    )(xn_bf, x1, a1, a2, tau, u_bf, vt_bf)


def _ple_final_kernel(x_ref, p_ref, gp_ref, wg_ref, wp_ref, gf_ref, y_ref):
    x = x_ref[...]
    gate = jax.nn.sigmoid(_dot(_rms(x, gp_ref[...]).astype(BF16), wg_ref[...]))
    x3 = x + gate * _dot(p_ref[...].astype(BF16), wp_ref[...])
    y_ref[...] = _rms(x3, gf_ref[...])


def _ple_final(x2, p2d, g_ple, wg_bf, wp_bf, g_final, tm):
    n, d = x2.shape
    pd = p2d.shape[1]
    return pl.pallas_call(
        _ple_final_kernel,
        grid=(n // tm,),
        in_specs=[pl.BlockSpec((tm, d), lambda i: (i, 0)),
                  pl.BlockSpec((tm, pd), lambda i: (i, 0)),
                  _full_spec((1, d)), _full_spec((d, d)), _full_spec((pd, d)), _full_spec((1, d))],
        out_specs=pl.BlockSpec((tm, d), lambda i: (i, 0)),
        out_shape=jax.ShapeDtypeStruct((n, d), F32),
        compiler_params=_cparams(("parallel",)),
        name="ple_final",
    )(x2, p2d, g_ple.reshape(1, d), wg_bf, wp_bf, g_final.reshape(1, d))


def _pick(n, pref):
    t = min(n, pref)
    while n % t:
        t //= 2
    return t


def _tiles(n_tokens, seq):
    return dict(
        tm=_pick(n_tokens, 512),
        tm_ple=_pick(n_tokens, 1024),
        tq=_pick(seq, 256),
        ts_lru=_pick(seq, 256),
        ts_sel=_pick(n_tokens, 512),
        tb=_pick(n_tokens, 512),
        ec=2048,
    )


def _tail(x1, xn_bf, s1, s2, p2d, wts, tiles):
    a1, a2, tau = _peer_select(s1, s2, tiles["ts_sel"])
    x2 = _peer_dense(xn_bf, x1, a1, a2, tau, wts["u"], wts["vt"], tiles["tb"], tiles["ec"])
    return _ple_final(x2, p2d, wts["g_ple"], wts["wg"], wts["wp"], wts["g_final"], tiles["tm_ple"])


def kernel(x_prompt, x_sample, cache_k, cache_v, state_conv, state_h, page_table, p_prompt, p_sample, norm_mix, w_in, sb_bias, conv_w, conv_b, gate_a_w, gate_a_b, gate_x_w, gate_x_b, lru_lambda, norm_attn_out, norm_lru_out, w_out, norm_ffn, peer_w_q, peer_keys1, peer_keys2, peer_u, peer_v, norm_ple, ple_w_gate, ple_w_proj, norm_final):
    depth = w_in.shape[0]
    assert depth == 1, "single-layer trunk"
    bp, seq, d = x_prompt.shape
    bd, dseq, _ = x_sample.shape
    assert dseq == 1, "one new token per sampled sequence"
    heads = sb_bias.shape[1]
    head_dim = cache_k.shape[-1]
    sbw = heads * head_dim
    lruw = lru_lambda.shape[1]
    peer_heads = peer_w_q.shape[2] // (2 * peer_keys1.shape[2])

    li = 0
    w_in_bf = w_in[li].astype(BF16)
    lw = _lru_weights(conv_w[li], conv_b[li], gate_a_w[li], gate_a_b[li], gate_x_w[li], gate_x_b[li],
                      lru_lambda[li])
    w_out_bf = w_out[li].astype(BF16)
    wq_bf = peer_w_q[li].astype(BF16)
    k1_bf = peer_keys1[li].astype(BF16)
    k2_bf = peer_keys2[li].astype(BF16)
    wts = dict(u=peer_u[li].astype(BF16), vt=peer_v[li].T.astype(BF16), g_ple=norm_ple[li],
               wg=ple_w_gate[li].astype(BF16), wp=ple_w_proj[li].astype(BF16), g_final=norm_final)

    n_p = bp * seq
    tp = _tiles(n_p, seq)
    xp2d = x_prompt.reshape(n_p, d)
    q, k_fm, v_fm, xl, gl = _in_proj(xp2d, norm_mix[li], w_in_bf, tp["tm"], seq=seq)
    o_att = _sb_prompt(q, k_fm, v_fm, sb_bias[li], bp, seq, head_dim, tp["tq"])
    conv0 = jnp.zeros((bp, CONV_WIDTH - 1, lruw), F32)
    h0 = jnp.zeros((bp, lruw), F32)
    o_lru, conv_p, h_p = _lru_prompt(xl, gl, conv0, h0, lw, bp, seq, tp["ts_lru"])
    x1, xn_bf, s1, s2 = _mix_peer_q(o_att, o_lru, xp2d, norm_attn_out[li], norm_lru_out[li], w_out_bf,
                                    norm_ffn[li], wq_bf, k1_bf, k2_bf, peer_heads, tp["tm"])
    y_p = _tail(x1, xn_bf, s1, s2, p_prompt[li].reshape(n_p, -1), wts, tp)

    ts_ = _tiles(bd, 1)
    xs2d = x_sample.reshape(bd, d)
    qs, ks, vs, xls, gls = _in_proj(xs2d, norm_mix[li], w_in_bf, ts_["tm"])
    o_att_s = _sb_sample(qs, sb_bias[li], cache_k[li], cache_v[li], page_table, heads, head_dim,
                         pp=math.gcd(page_table.shape[1], 32))
    o_lru_s, h_s = _lru_sample(xls, gls, state_conv[li], state_h[li], lw)
    conv_s = jnp.concatenate([state_conv[li][:, 1:], xls[:, None, :]], axis=1)
    x1s, xns_bf, s1s, s2s = _mix_peer_q(o_att_s, o_lru_s, xs2d, norm_attn_out[li], norm_lru_out[li], w_out_bf,
                                        norm_ffn[li], wq_bf, k1_bf, k2_bf, peer_heads, ts_["tm"])
    y_s = _tail(x1s, xns_bf, s1s, s2s, p_sample[li].reshape(bd, -1), wts, ts_)

    return (y_p.reshape(bp, seq, d), y_s.reshape(bd, 1, d),
            jnp.transpose(k_fm.reshape(bp, heads, head_dim, seq), (0, 3, 1, 2))[None],
            jnp.transpose(v_fm.reshape(bp, heads, head_dim, seq), (0, 3, 1, 2))[None],
            conv_p[None], h_p.reshape(1, bp, lruw),
            ks.reshape(1, bd, 1, heads, head_dim), vs.reshape(1, bd, 1, heads, head_dim),
            conv_s[None], h_s[None])
```
